```python
import math
import jax, jax.numpy as jnp
from jax import lax
import numpy as np

D_MODEL = 4096
BATCH = 4
SEQ = 2048
DEPTH = 2
DEC_BATCH = 32
DEC_SEQ = 8
PAST_LEN = 16384
PAGE_SIZE = 128

HEAD_DIM = 128
N_Q_HEADS = D_MODEL // HEAD_DIM
N_KV_HEADS = N_Q_HEADS // 4
Q_PER_KV = N_Q_HEADS // N_KV_HEADS
D_Q = N_Q_HEADS * HEAD_DIM
D_KV = N_KV_HEADS * HEAD_DIM
WINDOW = 128
ATTN_BLOCK = WINDOW
BUF_LEN = min(WINDOW, PAST_LEN)
D_CONV = D_MODEL // 2
CONV_WIDTH = 31
N_EXPERTS = 64
N_GROUPS = 8
EXPERTS_PER_GROUP = N_EXPERTS // N_GROUPS
TOP_K = 2
D_EXPERT = D_MODEL // 4
MOE_BLOCK = 64
EPS = 1e-6
SPLIT_POINTS = (D_Q, D_Q + D_KV, D_Q + 2 * D_KV, D_Q + 2 * D_KV + D_CONV,
                D_Q + 2 * D_KV + 2 * D_CONV, D_Q + 2 * D_KV + 2 * D_CONV + D_MODEL)
D_IN = D_Q + 2 * D_KV + 2 * D_CONV + 2 * D_MODEL

kernel_name = "hybrid_swa_sink_conformer_grouped_moe_step"


def rms_norm(x, g):
    xf = x.astype(jnp.float32)
    y = xf * lax.rsqrt(jnp.mean(xf * xf, axis=-1, keepdims=True) + EPS)
    return (y * g.astype(jnp.float32)).astype(x.dtype)


def layer_norm(x, g, b):
    xf = x.astype(jnp.float32)
    mu = jnp.mean(xf, axis=-1, keepdims=True)
    var = jnp.mean(jnp.square(xf - mu), axis=-1, keepdims=True)
    y = (xf - mu) * lax.rsqrt(var + EPS) * g.astype(jnp.float32) + b.astype(jnp.float32)
    return y.astype(x.dtype)


def window_attend(q, k, v, q_pos, k_pos, sink):
    s = jnp.einsum('bnqhgd,bnkhd->bnhgqk', q, k).astype(jnp.float32) * (1.0 / math.sqrt(HEAD_DIM))
    rel = q_pos[:, :, None] - k_pos[:, None, :]
    valid = (rel >= 0) & (rel <= WINDOW) & (k_pos[:, None, :] >= 0)
    s = jnp.where(valid[None, :, None, None, :, :], s, -jnp.inf)
    sink_b = sink.astype(jnp.float32).reshape(1, 1, N_KV_HEADS, Q_PER_KV, 1, 1)
    m = jnp.maximum(jnp.max(s, axis=-1, keepdims=True), sink_b)
    p = jnp.exp(s - m)
    denom = jnp.sum(p, axis=-1, keepdims=True) + jnp.exp(sink_b - m)
    return jnp.einsum('bnhgqk,bnkhd->bnqhgd', (p / denom).astype(v.dtype), v)


def attn_prompt(q, k, v, sink):
    B, S = q.shape[0], q.shape[1]
    nb = S // ATTN_BLOCK
    qb = q.reshape(B, nb, ATTN_BLOCK, N_KV_HEADS, Q_PER_KV, HEAD_DIM)
    kb = k.reshape(B, nb, ATTN_BLOCK, N_KV_HEADS, HEAD_DIM)
    vb = v.reshape(B, nb, ATTN_BLOCK, N_KV_HEADS, HEAD_DIM)
    pad = ((0, 0), (1, 0), (0, 0), (0, 0), (0, 0))
    kk = jnp.concatenate([jnp.pad(kb[:, :-1], pad), kb], axis=2)
    vv = jnp.concatenate([jnp.pad(vb[:, :-1], pad), vb], axis=2)
    q_pos = jnp.arange(S).reshape(nb, ATTN_BLOCK)
    k_pos = (jnp.arange(nb)[:, None] - 1) * ATTN_BLOCK + jnp.arange(2 * ATTN_BLOCK)[None, :]
    o = window_attend(qb, kk, vv, q_pos, k_pos, sink)
    return o.reshape(B, S, D_Q)


def attn_sample(q, kk, vv, sink):
    B, T = q.shape[0], q.shape[1]
    L = kk.shape[1]
    q_pos = PAST_LEN + jnp.arange(T)
    k_pos = PAST_LEN - BUF_LEN + jnp.arange(L)
    o = window_attend(q[:, None], kk[:, None], vv[:, None], q_pos[None], k_pos[None], sink)
    return o.reshape(B, T, D_Q)


def token_mixers(h, past, w_in, q_norm_g, k_norm_g, attn_sink, w_ao, dw_w, dw_b, cn_g, cn_b, w_co, w_out):
    B, T, _ = h.shape
    proj = h @ w_in
    q, k, v, u_val, u_gate, g_a, g_b = jnp.split(proj, SPLIT_POINTS, axis=-1)
    q = rms_norm(q.reshape(B, T, N_KV_HEADS, Q_PER_KV, HEAD_DIM), q_norm_g)
    k = rms_norm(k.reshape(B, T, N_KV_HEADS, HEAD_DIM), k_norm_g)
    v = v.reshape(B, T, N_KV_HEADS, HEAD_DIM)
    u = u_val * jax.nn.sigmoid(u_gate)
    if past is None:
        a = attn_prompt(q, k, v, attn_sink)
        kk, vv = k, v
        left = jnp.zeros((B, CONV_WIDTH - 1, D_CONV), u.dtype)
    else:
        k_buf, v_buf, left = past
        kk = jnp.concatenate([k_buf, k], axis=1)
        vv = jnp.concatenate([v_buf, v], axis=1)
        a = attn_sample(q, kk, vv, attn_sink)
    new_k = kk[:, -BUF_LEN:]
    new_v = vv[:, -BUF_LEN:]
    uc = jnp.concatenate([left, u], axis=1)
    new_u = uc[:, -(CONV_WIDTH - 1):]
    conv = lax.conv_general_dilated(uc, dw_w[:, None, :], window_strides=(1,), padding='VALID',
                                    dimension_numbers=('NWC', 'WIO', 'NWC'),
                                    feature_group_count=D_CONV) + dw_b
    conv = jax.nn.silu(layer_norm(conv, cn_g, cn_b))
    z = jax.nn.sigmoid(g_a) * (a @ w_ao) + jax.nn.sigmoid(g_b) * (conv @ w_co)
    return z @ w_out, new_k, new_v, new_u


def route(xf, router_w, router_b):
    n = xf.shape[0]
    s = jax.nn.sigmoid((xf @ router_w).astype(jnp.float32))
    sel = (s + router_b.astype(jnp.float32)).reshape(n, N_GROUPS, EXPERTS_PER_GROUP)
    group_score = jnp.sum(lax.top_k(sel, TOP_K)[0], axis=-1)
    grp = jnp.argmax(group_score, axis=-1).astype(jnp.int32)
    in_group = jnp.take_along_axis(sel, grp[:, None, None], axis=1)[:, 0]
    _, idx = lax.top_k(in_group, TOP_K)
    expert = grp[:, None] * EXPERTS_PER_GROUP + idx.astype(jnp.int32)
    w = jnp.take_along_axis(s, expert, axis=1)
    return expert, w / jnp.sum(w, axis=-1, keepdims=True)


def moe(h, router_w, router_b, w_gate, w_up, w_down):
    B, T, D = h.shape
    n = B * T
    xf = h.reshape(n, D)
    expert, wt = route(xf, router_w, router_b)
    a_n = n * TOP_K
    e_flat = expert.reshape(a_n)
    tok = jnp.repeat(jnp.arange(n, dtype=jnp.int32), TOP_K)
    w_flat = wt.reshape(a_n)
    order = jnp.argsort(e_flat)
    e_sorted = e_flat[order]
    counts = jnp.bincount(e_flat, length=N_EXPERTS)
    padded = (counts + MOE_BLOCK - 1) // MOE_BLOCK * MOE_BLOCK
    off = jnp.cumsum(counts) - counts
    pend = jnp.cumsum(padded)
    poff = pend - padded
    dest = poff[e_sorted] + jnp.arange(a_n) - off[e_sorted]
    n_blocks = -(-(a_n + N_EXPERTS * (MOE_BLOCK - 1)) // MOE_BLOCK)
    rows = n_blocks * MOE_BLOCK
    row_tok = jnp.full((rows,), n, jnp.int32).at[dest].set(tok[order])
    row_w = jnp.zeros((rows,), jnp.float32).at[dest].set(w_flat[order])
    block_exp = jnp.minimum(jnp.searchsorted(pend, jnp.arange(n_blocks) * MOE_BLOCK, side='right'),
                            N_EXPERTS - 1)
    xpad = jnp.concatenate([xf, jnp.zeros((1, D), xf.dtype)], axis=0)
    xb = xpad[row_tok].reshape(n_blocks, MOE_BLOCK, D)

    def expert_block(args):
        xblk, e = args
        return (jax.nn.silu(xblk @ w_gate[e]) * (xblk @ w_up[e])) @ w_down[e]

    yb = lax.map(expert_block, (xb, block_exp)).reshape(rows, D)
    y = jax.ops.segment_sum(yb * row_w[:, None].astype(yb.dtype), row_tok, num_segments=n + 1)[:n]
    return y.reshape(B, T, D)


def decoder_layer(x, c, past, norm1_g, norm2_g, w_ada, b_ada, w_in, q_norm_g, k_norm_g, attn_sink,
                  w_ao, dw_w, dw_b, cn_g, cn_b, w_co, w_out, router_w, router_b, w_gate, w_up, w_down):
    mod = jax.nn.silu(c) @ w_ada + b_ada
    sh1, sc1, g1, sh2, sc2, g2 = [m[:, None, :] for m in jnp.split(mod, 6, axis=-1)]
    h = rms_norm(x, norm1_g) * (1 + sc1) + sh1
    m_out, nk, nv, nu = token_mixers(h, past, w_in, q_norm_g, k_norm_g, attn_sink, w_ao,
                                     dw_w, dw_b, cn_g, cn_b, w_co, w_out)
    x = x + g1 * m_out
    h = rms_norm(x, norm2_g) * (1 + sc2) + sh2
    x = x + g2 * moe(h, router_w, router_b, w_gate, w_up, w_down)
    return x, nk, nv, nu


def setup_inputs(seed: int = 0) -> dict:
    key = jax.random.key(seed)
    ks = jax.random.split(key, 27)
    f32 = jnp.float32

    def nrm(k, shape, scale):
        return jax.random.normal(k, shape, f32) * scale

    def gain(k, shape):
        return 1.0 + 0.02 * jax.random.normal(k, shape, f32)

    return {
        "x_prompt": nrm(ks[0], (BATCH, SEQ, D_MODEL), 1.0),
        "x_sample": nrm(ks[1], (DEC_BATCH, DEC_SEQ, D_MODEL), 1.0),
        "cache_k": nrm(ks[2], (DEPTH, DEC_BATCH, BUF_LEN, N_KV_HEADS, HEAD_DIM), 1.0),
        "cache_v": nrm(ks[3], (DEPTH, DEC_BATCH, BUF_LEN, N_KV_HEADS, HEAD_DIM), 1.0),
        "state_conv": nrm(ks[4], (DEPTH, DEC_BATCH, CONV_WIDTH - 1, D_CONV), 0.5),
        "c_prompt": nrm(ks[5], (BATCH, D_MODEL), 1.0),
        "c_sample": nrm(ks[6], (DEC_BATCH, D_MODEL), 1.0),
        "router_w": nrm(ks[7], (D_MODEL, N_EXPERTS), D_MODEL ** -0.5),
        "router_b": nrm(ks[8], (N_EXPERTS,), 0.01),
        "norm1_g": gain(ks[9], (DEPTH, D_MODEL)),
        "norm2_g": gain(ks[10], (DEPTH, D_MODEL)),
        "w_ada": nrm(ks[11], (DEPTH, D_MODEL, 6 * D_MODEL), 0.5 * D_MODEL ** -0.5),
        "b_ada": nrm(ks[12], (DEPTH, 6 * D_MODEL), 0.02),
        "w_in": nrm(ks[13], (DEPTH, D_MODEL, D_IN), D_MODEL ** -0.5),
        "q_norm_g": gain(ks[14], (DEPTH, HEAD_DIM)),
        "k_norm_g": gain(ks[15], (DEPTH, HEAD_DIM)),
        "attn_sink": nrm(ks[16], (DEPTH, N_Q_HEADS), 1.0),
        "w_ao": nrm(ks[17], (DEPTH, D_Q, D_MODEL), D_Q ** -0.5),
        "dw_w": nrm(ks[18], (DEPTH, CONV_WIDTH, D_CONV), CONV_WIDTH ** -0.5),
        "dw_b": nrm(ks[19], (DEPTH, D_CONV), 0.02),
        "cn_g": gain(ks[20], (DEPTH, D_CONV)),
        "cn_b": nrm(ks[21], (DEPTH, D_CONV), 0.02),
        "w_co": nrm(ks[22], (DEPTH, D_CONV, D_MODEL), D_CONV ** -0.5),
        "w_out": nrm(ks[23], (DEPTH, D_MODEL, D_MODEL), D_MODEL ** -0.5),
        "moe_w_gate": nrm(ks[24], (DEPTH, N_EXPERTS, D_MODEL, D_EXPERT), D_MODEL ** -0.5),
        "moe_w_up": nrm(ks[25], (DEPTH, N_EXPERTS, D_MODEL, D_EXPERT), D_MODEL ** -0.5),
        "moe_w_down": nrm(ks[26], (DEPTH, N_EXPERTS, D_EXPERT, D_MODEL), D_EXPERT ** -0.5),
    }


def reference(x_prompt, x_sample, cache_k, cache_v, state_conv, c_prompt, c_sample, router_w, router_b,
              norm1_g, norm2_g, w_ada, b_ada, w_in, q_norm_g, k_norm_g, attn_sink, w_ao, dw_w, dw_b,
              cn_g, cn_b, w_co, w_out, moe_w_gate, moe_w_up, moe_w_down):
    xp, xs = x_prompt, x_sample
    kp, vp, up, ksl, vsl, usl = [], [], [], [], [], []
    for l in range(DEPTH):
        lw = (norm1_g[l], norm2_g[l], w_ada[l], b_ada[l], w_in[l], q_norm_g[l], k_norm_g[l], attn_sink[l],
              w_ao[l], dw_w[l], dw_b[l], cn_g[l], cn_b[l], w_co[l], w_out[l], router_w, router_b,
              moe_w_gate[l], moe_w_up[l], moe_w_down[l])
        xp, nk, nv, nu = decoder_layer(xp, c_prompt, None, *lw)
        xs, sk, sv, su = decoder_layer(xs, c_sample, (cache_k[l], cache_v[l], state_conv[l]), *lw)
        kp.append(nk); vp.append(nv); up.append(nu)
        ksl.append(sk); vsl.append(sv); usl.append(su)
    cache_k_prompt = jnp.stack(kp)
    cache_v_prompt = jnp.stack(vp)
    state_conv_prompt = jnp.stack(up)
    cache_k_sample = jnp.stack(ksl)
    cache_v_sample = jnp.stack(vsl)
    state_conv_sample = jnp.stack(usl)
    return (xp, xs, cache_k_prompt, cache_v_prompt, state_conv_prompt, cache_k_sample, cache_v_sample, state_conv_sample)
```

```python
import functools
import math

import jax
import jax.numpy as jnp
from jax import lax
from jax.experimental import pallas as pl
from jax.experimental.pallas import tpu as pltpu

EPS = 1e-6
HEAD_DIM = 128
Q_PER_KV = 4
CONV_WIDTH = 31
N_GROUPS = 8
TOP_K = 2

V7X_SUBLANES = 8
V7X_LANES = 128
V7X_VMEM_BYTES = 64 * 1024 * 1024
_MIB = 1024 * 1024
_BF16_ROWS = 2 * V7X_SUBLANES
MXU_DTYPE = jnp.bfloat16

CONV_HALO = 32
CONV_LEAD = CONV_HALO - (CONV_WIDTH - 1)
MOE_BLOCK_ROWS = 512
MOE_PHASE_STEPS = 4


def _bf16(x):
    return x.astype(MXU_DTYPE)


def _pick_tile(n, target, align):
    best = None
    for t in range(align, min(n, target) + 1, align):
        if n % t == 0:
            best = t
    if best is None:
        raise ValueError(f"no tile for n={n} target={target} align={align}")
    return best


def _exact_div(a, b):
    q, r = divmod(a, b)
    if r:
        raise ValueError(f"{a} is not a multiple of {b}")
    return q


def _params(semantics, block_bytes):
    need = 2 * block_bytes + 8 * _MIB
    limit = int(min(max(need, 16 * _MIB), V7X_VMEM_BYTES - 8 * _MIB))
    return pltpu.CompilerParams(dimension_semantics=semantics, vmem_limit_bytes=limit)


def _adaln_kernel(c_ref, w_ref, b_ref, o_ref):
    c = c_ref[...]
    a = _bf16(c * jax.nn.sigmoid(c))
    o_ref[...] = jnp.dot(a, _bf16(w_ref[...]), preferred_element_type=jnp.float32) + b_ref[...]


def _adaln(c_all, w_ada, b_ada):
    depth, d, n = w_ada.shape
    rows = c_all.shape[0]
    tn = _pick_tile(n, 1024, V7X_LANES)
    return pl.pallas_call(
        _adaln_kernel,
        grid=(depth, n // tn),
        in_specs=[pl.BlockSpec((rows, d), lambda l, j: (0, 0)),
                  pl.BlockSpec((None, d, tn), lambda l, j: (l, 0, j)),
                  pl.BlockSpec((None, 1, tn), lambda l, j: (l, 0, j))],
        out_specs=pl.BlockSpec((None, rows, tn), lambda l, j: (l, 0, j)),
        out_shape=jax.ShapeDtypeStruct((depth, rows, n), jnp.float32),
        compiler_params=_params(("arbitrary", "arbitrary"), d * tn * 6 + rows * d * 4),
        name="adaln",
    )(c_all, w_ada, b_ada.reshape(depth, 1, n))


def _modulated_norm_rows(x_ref, g_ref, sc_ref, sh_ref, row0, n_rows):
    g = g_ref[...]
    parts = []
    for r in range(row0, row0 + n_rows, V7X_SUBLANES):
        x = x_ref[r:r + V7X_SUBLANES, :]
        gi = r // V7X_SUBLANES
        y = x * lax.rsqrt(jnp.mean(x * x, axis=-1, keepdims=True) + EPS) * g
        parts.append(y * (1.0 + sc_ref[gi:gi + 1, :]) + sh_ref[gi:gi + 1, :])
    return parts[0] if len(parts) == 1 else jnp.concatenate(parts, axis=0)


def _norm_mod_kernel(x_ref, g_ref, sc_ref, sh_ref, o_ref):
    for r in range(0, o_ref.shape[0], _BF16_ROWS):
        o_ref[r:r + _BF16_ROWS, :] = _bf16(_modulated_norm_rows(x_ref, g_ref, sc_ref, sh_ref, r, _BF16_ROWS))


def _norm_mod(x, g, modg, shift_col, scale_col):
    rows, d = x.shape
    tr = _pick_tile(rows, 256, _BF16_ROWS)
    tg = tr // V7X_SUBLANES
    return pl.pallas_call(
        _norm_mod_kernel,
        grid=(rows // tr,),
        in_specs=[pl.BlockSpec((tr, d), lambda i: (i, 0)),
                  pl.BlockSpec((1, d), lambda i: (0, 0)),
                  pl.BlockSpec((tg, d), lambda i: (i, scale_col)),
                  pl.BlockSpec((tg, d), lambda i: (i, shift_col))],
        out_specs=pl.BlockSpec((tr, d), lambda i: (i, 0)),
        out_shape=jax.ShapeDtypeStruct((rows, d), MXU_DTYPE),
        compiler_params=_params(("arbitrary",), tr * d * 8),
        name="norm_mod",
    )(x, g.reshape(1, d), modg, modg)


def _matmul_kernel(x_ref, w_ref, o_ref):
    o_ref[...] = jnp.dot(x_ref[...], _bf16(w_ref[...]), preferred_element_type=jnp.float32)


def _in_proj(h, w_in, layer):
    rows, k = h.shape
    n = w_in.shape[2]
    tm = _pick_tile(rows, 768, _BF16_ROWS)
    tn = _pick_tile(n, 512, V7X_LANES)
    return pl.pallas_call(
        _matmul_kernel,
        grid=(n // tn, rows // tm),
        in_specs=[pl.BlockSpec((tm, k), lambda j, i: (i, 0)),
                  pl.BlockSpec((None, k, tn), lambda j, i: (layer, 0, j))],
        out_specs=pl.BlockSpec((tm, tn), lambda j, i: (i, j)),
        out_shape=jax.ShapeDtypeStruct((rows, n), jnp.float32),
        compiler_params=_params(("arbitrary", "arbitrary"), tm * k * 2 + k * tn * 6 + tm * tn * 4),
        name="in_proj",
    )(h, w_in)


def _head_rms(x, g):
    return x * lax.rsqrt(jnp.mean(x * x, axis=-1, keepdims=True) + EPS) * g


def _window_mask(n_q, window, older_present):
    n_rows = Q_PER_KV * n_q
    row = lax.broadcasted_iota(jnp.int32, (n_rows, 2 * window), 0) % n_q
    col = lax.broadcasted_iota(jnp.int32, (n_rows, 2 * window), 1)
    return (col >= row) & (col <= row + window) & ((col >= window) | older_present)


def _kv_group_attention(q_heads, k_all, v_all, sink_vals, valid, n_q):
    q_all = jnp.concatenate(q_heads, axis=0)
    sink_col = jnp.concatenate([jnp.full((n_q, 1), sv, jnp.float32) for sv in sink_vals], axis=0)
    s = lax.dot_general(_bf16(q_all), _bf16(k_all), (((1,), (1,)), ((), ())),
                        preferred_element_type=jnp.float32) * (1.0 / math.sqrt(HEAD_DIM))
    s = jnp.where(valid, s, -jnp.inf)
    m = jnp.maximum(jnp.max(s, axis=-1, keepdims=True), sink_col)
    p = jnp.exp(s - m)
    denom = jnp.sum(p, axis=-1, keepdims=True) + jnp.exp(sink_col - m)
    return jnp.dot(_bf16(p / denom), _bf16(v_all), preferred_element_type=jnp.float32)


def _attn_prompt_kernel(sink_ref, q_ref, kp_ref, kc_ref, vp_ref, vc_ref, qg_ref, kg_ref, o_ref, kn_ref,
                        *, n_kv, window):
    qg = qg_ref[...]
    kg = kg_ref[...]
    valid = _window_mask(window, window, pl.program_id(1) > 0)
    for h in range(n_kv):
        ksl = slice(h * HEAD_DIM, (h + 1) * HEAD_DIM)
        k_cur = _head_rms(kc_ref[:, ksl], kg)
        kn_ref[:, ksl] = k_cur
        k_all = jnp.concatenate([_head_rms(kp_ref[:, ksl], kg), k_cur], axis=0)
        v_all = jnp.concatenate([vp_ref[:, ksl], vc_ref[:, ksl]], axis=0)
        heads = [h * Q_PER_KV + g for g in range(Q_PER_KV)]
        q_heads = [_head_rms(q_ref[:, hq * HEAD_DIM:(hq + 1) * HEAD_DIM], qg) for hq in heads]
        o = _kv_group_attention(q_heads, k_all, v_all, [sink_ref[hq] for hq in heads], valid, window)
        for g, hq in enumerate(heads):
            o_ref[:, hq * HEAD_DIM:(hq + 1) * HEAD_DIM] = _bf16(o[g * window:(g + 1) * window])


def _attn_prompt(proj, sink, q_g, k_g, *, batch, seq, total_rows, d_q, d_kv, window):
    n_kv = d_kv // HEAD_DIM
    nb = _exact_div(seq, window)
    k_col = _exact_div(d_q, d_kv)
    v_col = k_col + 1
    kern = functools.partial(_attn_prompt_kernel, n_kv=n_kv, window=window)

    def cur(b, i, s):
        return b * nb + i

    def prev(b, i, s):
        return b * nb + jnp.maximum(i - 1, 0)

    grid_spec = pltpu.PrefetchScalarGridSpec(
        num_scalar_prefetch=1,
        grid=(batch, nb),
        in_specs=[pl.BlockSpec((window, d_q), lambda b, i, s: (cur(b, i, s), 0)),
                  pl.BlockSpec((window, d_kv), lambda b, i, s: (prev(b, i, s), k_col)),
                  pl.BlockSpec((window, d_kv), lambda b, i, s: (cur(b, i, s), k_col)),
                  pl.BlockSpec((window, d_kv), lambda b, i, s: (prev(b, i, s), v_col)),
                  pl.BlockSpec((window, d_kv), lambda b, i, s: (cur(b, i, s), v_col)),
                  pl.BlockSpec((1, HEAD_DIM), lambda b, i, s: (0, 0)),
                  pl.BlockSpec((1, HEAD_DIM), lambda b, i, s: (0, 0))],
        out_specs=[pl.BlockSpec((window, d_q), lambda b, i, s: (cur(b, i, s), 0)),
                   pl.BlockSpec((None, window, d_kv), lambda b, i, s: (b, 0, 0))],
    )
    return pl.pallas_call(
        kern,
        grid_spec=grid_spec,
        out_shape=[jax.ShapeDtypeStruct((total_rows, d_q), MXU_DTYPE),
                   jax.ShapeDtypeStruct((batch, window, d_kv), jnp.float32)],
        compiler_params=_params(("arbitrary", "arbitrary"), window * (d_q * 6 + d_kv * 20)),
        name="attn_prompt",
    )(sink, proj, proj, proj, proj, proj, q_g.reshape(1, HEAD_DIM), k_g.reshape(1, HEAD_DIM))


def _attn_sample_kernel(sink_ref, a_in_ref, q_ref, kn_ref, vn_ref, kc_ref, vc_ref, qg_ref, kg_ref,
                        o_ref, knew_ref, *, n_kv, window, t_new, seqs):
    del a_in_ref
    qg = qg_ref[...]
    kg = kg_ref[...]
    valid = _window_mask(t_new, window, True)
    fill = jnp.zeros((window - t_new, HEAD_DIM), jnp.float32)
    for sq in range(seqs):
        rows = slice(sq * t_new, (sq + 1) * t_new)
        for h in range(n_kv):
            ksl = slice(h * HEAD_DIM, (h + 1) * HEAD_DIM)
            k_new = _head_rms(kn_ref[rows, ksl], kg)
            knew_ref[rows, ksl] = k_new
            k_all = jnp.concatenate([kc_ref[sq, :, ksl], k_new, fill], axis=0)
            v_all = jnp.concatenate([vc_ref[sq, :, ksl], vn_ref[rows, ksl], fill], axis=0)
            heads = [h * Q_PER_KV + g for g in range(Q_PER_KV)]
            q_heads = [_head_rms(q_ref[rows, hq * HEAD_DIM:(hq + 1) * HEAD_DIM], qg) for hq in heads]
            o = _kv_group_attention(q_heads, k_all, v_all, [sink_ref[hq] for hq in heads], valid, t_new)
            for g, hq in enumerate(heads):
                o_ref[rows, hq * HEAD_DIM:(hq + 1) * HEAD_DIM] = _bf16(o[g * t_new:(g + 1) * t_new])


def _attn_sample(a, proj, cache_k_l, cache_v_l, sink, q_g, k_g, *, prompt_rows, dec_batch, t_new,
                 d_q, d_kv, window):
    n_kv = d_kv // HEAD_DIM
    seqs = _exact_div(_BF16_ROWS, t_new) if _BF16_ROWS % t_new == 0 and dec_batch % (_BF16_ROWS // t_new) == 0 else 1
    tr = seqs * t_new
    row0 = _exact_div(prompt_rows, tr)
    k_col = _exact_div(d_q, d_kv)
    kern = functools.partial(_attn_sample_kernel, n_kv=n_kv, window=window, t_new=t_new, seqs=seqs)
    kc = cache_k_l.reshape(dec_batch, window, d_kv)
    vc = cache_v_l.reshape(dec_batch, window, d_kv)
    grid_spec = pltpu.PrefetchScalarGridSpec(
        num_scalar_prefetch=1,
        grid=(dec_batch // seqs,),
        in_specs=[pl.BlockSpec(memory_space=pl.ANY),
                  pl.BlockSpec((tr, d_q), lambda i, s: (row0 + i, 0)),
                  pl.BlockSpec((tr, d_kv), lambda i, s: (row0 + i, k_col)),
                  pl.BlockSpec((tr, d_kv), lambda i, s: (row0 + i, k_col + 1)),
                  pl.BlockSpec((seqs, window, d_kv), lambda i, s: (i, 0, 0)),
                  pl.BlockSpec((seqs, window, d_kv), lambda i, s: (i, 0, 0)),
                  pl.BlockSpec((1, HEAD_DIM), lambda i, s: (0, 0)),
                  pl.BlockSpec((1, HEAD_DIM), lambda i, s: (0, 0))],
        out_specs=[pl.BlockSpec((tr, d_q), lambda i, s: (row0 + i, 0)),
                   pl.BlockSpec((tr, d_kv), lambda i, s: (i, 0))],
    )
    return pl.pallas_call(
        kern,
        grid_spec=grid_spec,
        out_shape=[jax.ShapeDtypeStruct(a.shape, a.dtype),
                   jax.ShapeDtypeStruct((dec_batch * t_new, d_kv), jnp.float32)],
        input_output_aliases={1: 0},
        compiler_params=_params(("arbitrary",), tr * (d_q * 6 + d_kv * 12) + seqs * window * d_kv * 8),
        name="attn_sample",
    )(sink, a, proj, proj, proj, kc, vc, q_g.reshape(1, HEAD_DIM), k_g.reshape(1, HEAD_DIM))


def _depthwise_conv(uext_ref, base, n_rows, w_ref, b_ref, conv_ref, conv_base, *, row_chunk, ch_chunk):
    d_conv = conv_ref.shape[-1]
    for r0 in range(0, n_rows, row_chunk):
        for c0 in range(0, d_conv, ch_chunk):
            cs = slice(c0, c0 + ch_chunk)
            acc = jnp.broadcast_to(b_ref[:, cs], (row_chunk, ch_chunk))
            for j in range(CONV_WIDTH):
                start = base + r0 + CONV_LEAD + j
                acc = acc + uext_ref[start:start + row_chunk, cs] * w_ref[j:j + 1, cs]
            conv_ref[conv_base + r0:conv_base + r0 + row_chunk, cs] = acc


def _layer_norm_swish(x, g, b):
    mu = jnp.mean(x, axis=-1, keepdims=True)
    xc = x - mu
    var = jnp.mean(xc * xc, axis=-1, keepdims=True)
    y = xc * lax.rsqrt(var + EPS) * g + b
    return y * jax.nn.sigmoid(y)


def _conv_prompt_kernel(uv_ref, ug_ref, w_ref, b_ref, g_ref, bb_ref, o_ref, st_ref, uext_ref, conv_ref,
                        *, tt):
    @pl.when(pl.program_id(1) == 0)
    def _():
        uext_ref[0:CONV_HALO, :] = jnp.zeros((CONV_HALO, uext_ref.shape[1]), jnp.float32)

    uext_ref[CONV_HALO:CONV_HALO + tt, :] = uv_ref[...] * jax.nn.sigmoid(ug_ref[...])
    _depthwise_conv(uext_ref, 0, tt, w_ref, b_ref, conv_ref, 0, row_chunk=32, ch_chunk=256)
    o_ref[...] = _bf16(_layer_norm_swish(conv_ref[...], g_ref[...], bb_ref[...]))
    st_ref[...] = uext_ref[tt + CONV_LEAD:tt + CONV_HALO, :]
    uext_ref[0:CONV_HALO, :] = uext_ref[tt:tt + CONV_HALO, :]


def _conv_prompt(proj, dw_w, dw_b, cn_g, cn_b, *, batch, seq, total_rows, d_q, d_kv, d_conv):
    tt = _pick_tile(seq, 256, 32)
    nt = seq // tt
    uv_col = _exact_div(d_q + 2 * d_kv, d_conv)
    kern = functools.partial(_conv_prompt_kernel, tt=tt)
    return pl.pallas_call(
        kern,
        grid=(batch, nt),
        in_specs=[pl.BlockSpec((tt, d_conv), lambda b, t: (b * nt + t, uv_col)),
                  pl.BlockSpec((tt, d_conv), lambda b, t: (b * nt + t, uv_col + 1)),
                  pl.BlockSpec((CONV_WIDTH, d_conv), lambda b, t: (0, 0)),
                  pl.BlockSpec((1, d_conv), lambda b, t: (0, 0)),
                  pl.BlockSpec((1, d_conv), lambda b, t: (0, 0)),
                  pl.BlockSpec((1, d_conv), lambda b, t: (0, 0))],
        out_specs=[pl.BlockSpec((tt, d_conv), lambda b, t: (b * nt + t, 0)),
                   pl.BlockSpec((None, CONV_WIDTH - 1, d_conv), lambda b, t: (b, 0, 0))],
        out_shape=[jax.ShapeDtypeStruct((total_rows, d_conv), MXU_DTYPE),
                   jax.ShapeDtypeStruct((batch, CONV_WIDTH - 1, d_conv), jnp.float32)],
        scratch_shapes=[pltpu.VMEM((tt + CONV_HALO, d_conv), jnp.float32),
                        pltpu.VMEM((tt, d_conv), jnp.float32)],
        compiler_params=_params(("arbitrary", "arbitrary"), tt * d_conv * 18),
        name="conv_prompt",
    )(proj, proj, dw_w, dw_b.reshape(1, d_conv), cn_g.reshape(1, d_conv), cn_b.reshape(1, d_conv))


def _conv_sample_kernel(cv_in_ref, uv_ref, ug_ref, st_ref, w_ref, b_ref, g_ref, bb_ref, o_ref, nst_ref,
                        uext_ref, conv_ref, *, t_new, seqs):
    del cv_in_ref
    ext = CONV_HALO + t_new
    for sq in range(seqs):
        base = sq * ext
        rows = slice(sq * t_new, (sq + 1) * t_new)
        uext_ref[base:base + CONV_LEAD, :] = jnp.zeros((CONV_LEAD, uext_ref.shape[1]), jnp.float32)
        uext_ref[base + CONV_LEAD:base + CONV_HALO, :] = st_ref[sq]
        uext_ref[base + CONV_HALO:base + ext, :] = uv_ref[rows, :] * jax.nn.sigmoid(ug_ref[rows, :])
        nst_ref[sq] = uext_ref[base + ext - (CONV_WIDTH - 1):base + ext, :]
        _depthwise_conv(uext_ref, base, t_new, w_ref, b_ref, conv_ref, sq * t_new,
                        row_chunk=t_new, ch_chunk=256)
    o_ref[...] = _bf16(_layer_norm_swish(conv_ref[...], g_ref[...], bb_ref[...]))


def _conv_sample(cv, proj, state_l, dw_w, dw_b, cn_g, cn_b, *, prompt_rows, dec_batch, t_new,
                 d_q, d_kv, d_conv):
    seqs = _exact_div(_BF16_ROWS, t_new) if _BF16_ROWS % t_new == 0 and dec_batch % (_BF16_ROWS // t_new) == 0 else 1
    tr = seqs * t_new
    row0 = _exact_div(prompt_rows, tr)
    uv_col = _exact_div(d_q + 2 * d_kv, d_conv)
    kern = functools.partial(_conv_sample_kernel, t_new=t_new, seqs=seqs)
    return pl.pallas_call(
        kern,
        grid=(dec_batch // seqs,),
        in_specs=[pl.BlockSpec(memory_space=pl.ANY),
                  pl.BlockSpec((tr, d_conv), lambda i: (row0 + i, uv_col)),
                  pl.BlockSpec((tr, d_conv), lambda i: (row0 + i, uv_col + 1)),
                  pl.BlockSpec((seqs, CONV_WIDTH - 1, d_conv), lambda i: (i, 0, 0)),
                  pl.BlockSpec((CONV_WIDTH, d_conv), lambda i: (0, 0)),
                  pl.BlockSpec((1, d_conv), lambda i: (0, 0)),
                  pl.BlockSpec((1, d_conv), lambda i: (0, 0)),
                  pl.BlockSpec((1, d_conv), lambda i: (0, 0))],
        out_specs=[pl.BlockSpec((tr, d_conv), lambda i: (row0 + i, 0)),
                   pl.BlockSpec((seqs, CONV_WIDTH - 1, d_conv), lambda i: (i, 0, 0))],
        out_shape=[jax.ShapeDtypeStruct(cv.shape, cv.dtype),
                   jax.ShapeDtypeStruct((dec_batch, CONV_WIDTH - 1, d_conv), jnp.float32)],
        input_output_aliases={0: 0},
        scratch_shapes=[pltpu.VMEM((seqs * (CONV_HALO + t_new), d_conv), jnp.float32),
                        pltpu.VMEM((tr, d_conv), jnp.float32)],
        compiler_params=_params(("arbitrary",), (tr * 12 + seqs * 64 * 4 + 40 * 4) * d_conv),
        name="conv_sample",
    )(cv, proj, proj, state_l, dw_w, dw_b.reshape(1, d_conv), cn_g.reshape(1, d_conv), cn_b.reshape(1, d_conv))


def _merge_kernel(a_ref, cv_ref, wao_ref, wco_ref, ga_ref, gb_ref, o_ref):
    pa = jnp.dot(a_ref[...], _bf16(wao_ref[...]), preferred_element_type=jnp.float32)
    pc = jnp.dot(cv_ref[...], _bf16(wco_ref[...]), preferred_element_type=jnp.float32)
    o_ref[...] = _bf16(jax.nn.sigmoid(ga_ref[...]) * pa + jax.nn.sigmoid(gb_ref[...]) * pc)


def _merge(a, cv, proj, w_ao, w_co, layer, *, ga_off):
    rows, d_q = a.shape
    d_conv = cv.shape[1]
    d = w_ao.shape[2]
    tm = _pick_tile(rows, 384, _BF16_ROWS)
    tn = _pick_tile(d, 512, V7X_LANES)
    ga_col = _exact_div(ga_off, tn)
    gb_col = ga_col + d // tn
    blk = tm * (d_q + d_conv) * 2 + (d_q + d_conv) * tn * 6 + tm * tn * 10
    return pl.pallas_call(
        _merge_kernel,
        grid=(d // tn, rows // tm),
        in_specs=[pl.BlockSpec((tm, d_q), lambda j, i: (i, 0)),
                  pl.BlockSpec((tm, d_conv), lambda j, i: (i, 0)),
                  pl.BlockSpec((None, d_q, tn), lambda j, i: (layer, 0, j)),
                  pl.BlockSpec((None, d_conv, tn), lambda j, i: (layer, 0, j)),
                  pl.BlockSpec((tm, tn), lambda j, i: (i, ga_col + j)),
                  pl.BlockSpec((tm, tn), lambda j, i: (i, gb_col + j))],
        out_specs=pl.BlockSpec((tm, tn), lambda j, i: (i, j)),
        out_shape=jax.ShapeDtypeStruct((rows, d), MXU_DTYPE),
        compiler_params=_params(("arbitrary", "arbitrary"), blk),
        name="merge",
    )(a, cv, w_ao, w_co, proj, proj)


def _out_proj_kernel(z_ref, w_ref, x_ref, g_ref, o_ref):
    m = jnp.dot(z_ref[...], _bf16(w_ref[...]), preferred_element_type=jnp.float32)
    for r in range(0, m.shape[0], V7X_SUBLANES):
        gi = r // V7X_SUBLANES
        rs = slice(r, r + V7X_SUBLANES)
        o_ref[rs, :] = x_ref[rs, :] + g_ref[gi:gi + 1, :] * m[rs]


def _out_proj(z, w_out, x, modg, layer, *, gate_col):
    rows, d = x.shape
    tm = _pick_tile(rows, 768, _BF16_ROWS)
    tn = _pick_tile(d, 512, V7X_LANES)
    gcol = gate_col * (d // tn)
    blk = tm * d * 2 + d * tn * 6 + tm * tn * 9
    return pl.pallas_call(
        _out_proj_kernel,
        grid=(d // tn, rows // tm),
        in_specs=[pl.BlockSpec((tm, d), lambda j, i: (i, 0)),
                  pl.BlockSpec((None, d, tn), lambda j, i: (layer, 0, j)),
                  pl.BlockSpec((tm, tn), lambda j, i: (i, j)),
                  pl.BlockSpec((tm // V7X_SUBLANES, tn), lambda j, i: (i, gcol + j))],
        out_specs=pl.BlockSpec((tm, tn), lambda j, i: (i, j)),
        out_shape=jax.ShapeDtypeStruct((rows, d), jnp.float32),
        compiler_params=_params(("arbitrary", "arbitrary"), blk),
        name="out_proj",
    )(z, w_out, x, modg)


def _first_argmax(x, axis):
    m = jnp.max(x, axis=axis, keepdims=True)
    idx = lax.broadcasted_iota(jnp.int32, x.shape, axis)
    first = jnp.min(jnp.where(x == m, idx, x.shape[axis]), axis=axis, keepdims=True)
    return m, first


def _router_kernel(x_ref, g_ref, sc_ref, sh_ref, rwt_ref, rb_ref, h_ref, eid_ref, wt_ref, *, n_experts):
    tr = x_ref.shape[0]
    for r in range(0, tr, V7X_SUBLANES):
        h_ref[r:r + V7X_SUBLANES, :] = _modulated_norm_rows(x_ref, g_ref, sc_ref, sh_ref, r, V7X_SUBLANES)
    per_group = n_experts // N_GROUPS
    logits = lax.dot_general(rwt_ref[...], h_ref[...], (((1,), (1,)), ((), ())),
                             precision=lax.Precision.HIGHEST,
                             preferred_element_type=jnp.float32)
    s = jax.nn.sigmoid(logits)
    sel = (s + rb_ref[...]).reshape(N_GROUPS, per_group, tr)
    m1, i1 = _first_argmax(sel, 1)
    e_iota = lax.broadcasted_iota(jnp.int32, sel.shape, 1)
    m2 = jnp.max(jnp.where(e_iota == i1, -jnp.inf, sel), axis=1, keepdims=True)
    _, grp = _first_argmax(m1 + m2, 0)
    g_iota = lax.broadcasted_iota(jnp.int32, sel.shape, 0)
    in_group = jnp.max(jnp.where(g_iota == grp, sel, -jnp.inf), axis=0, keepdims=True)
    _, j1 = _first_argmax(in_group, 1)
    j_iota = lax.broadcasted_iota(jnp.int32, in_group.shape, 1)
    _, j2 = _first_argmax(jnp.where(j_iota == j1, -jnp.inf, in_group), 1)
    e1 = (grp * per_group + j1).reshape(1, tr)
    e2 = (grp * per_group + j2).reshape(1, tr)
    x_iota = lax.broadcasted_iota(jnp.int32, s.shape, 0)
    w1 = jnp.sum(jnp.where(x_iota == e1, s, 0.0), axis=0, keepdims=True)
    w2 = jnp.sum(jnp.where(x_iota == e2, s, 0.0), axis=0, keepdims=True)
    tot = w1 + w2
    eid_ref[...] = jnp.concatenate([e1, e2], axis=0)
    wt_ref[...] = jnp.concatenate([w1 / tot, w2 / tot], axis=0)


def _norm_router(x, g, modg, router_w, router_b, *, shift_col, scale_col):
    rows, d = x.shape
    n_experts = router_w.shape[1]
    tr = _pick_tile(rows, 256, V7X_LANES) if rows % V7X_LANES == 0 else rows
    tg = tr // V7X_SUBLANES
    kern = functools.partial(_router_kernel, n_experts=n_experts)
    return pl.pallas_call(
        kern,
        grid=(rows // tr,),
        in_specs=[pl.BlockSpec((tr, d), lambda i: (i, 0)),
                  pl.BlockSpec((1, d), lambda i: (0, 0)),
                  pl.BlockSpec((tg, d), lambda i: (i, scale_col)),
                  pl.BlockSpec((tg, d), lambda i: (i, shift_col)),
                  pl.BlockSpec((n_experts, d), lambda i: (0, 0)),
                  pl.BlockSpec((n_experts, 1), lambda i: (0, 0))],
        out_specs=[pl.BlockSpec((tr, d), lambda i: (i, 0)),
                   pl.BlockSpec((TOP_K, tr), lambda i: (0, i)),
                   pl.BlockSpec((TOP_K, tr), lambda i: (0, i))],
        out_shape=[jax.ShapeDtypeStruct((rows, d), jnp.float32),
                   jax.ShapeDtypeStruct((TOP_K, rows), jnp.int32),
                   jax.ShapeDtypeStruct((TOP_K, rows), jnp.float32)],
        compiler_params=_params(("arbitrary",), tr * d * 16 + n_experts * d * 4),
        name="norm_router",
    )(x, g.reshape(1, d), modg, modg, router_w.T, router_b.reshape(n_experts, 1))


def _dispatch_plan(eid, n_experts, block_rows):
    top_k, rows = eid.shape
    n_assign = top_k * rows
    n_blocks = -(-(n_assign + n_experts * (block_rows - 1)) // block_rows)
    e_flat = eid.reshape(n_assign)
    onehot = (e_flat[:, None] == jnp.arange(n_experts, dtype=jnp.int32)[None, :]).astype(jnp.int32)
    rank = jnp.sum((jnp.cumsum(onehot, axis=0) - onehot) * onehot, axis=1)
    counts = jnp.sum(onehot, axis=0)
    padded = (counts + block_rows - 1) // block_rows * block_rows
    pend = jnp.cumsum(padded)
    poff = pend - padded
    dest = poff[e_flat] + rank
    tok = jnp.tile(jnp.arange(rows, dtype=jnp.int32), top_k)
    row_tok = jnp.zeros((n_blocks * block_rows,), jnp.int32).at[dest].set(tok)
    n_used = pend[-1] // block_rows
    blk = jnp.minimum(jnp.arange(n_blocks, dtype=jnp.int32), n_used - 1)
    blk_exp = jnp.minimum(jnp.searchsorted(pend, blk * block_rows, side='right'), n_experts - 1)
    return row_tok, dest.reshape(top_k, rows).astype(jnp.int32), blk_exp.astype(jnp.int32), n_used.reshape(1).astype(jnp.int32)


def _gather_rows_kernel(tok_ref, nsteps_ref, src_ref, o_ref, sem, *, rows_per_step):
    i = pl.program_id(0)

    def copy(r):
        tok = tok_ref[i * rows_per_step + r]
        return pltpu.make_async_copy(src_ref.at[pl.ds(tok, 1)], o_ref.at[pl.ds(r, 1)], sem)

    @pl.when(i < nsteps_ref[0])
    def _():
        def start(r, c):
            copy(r).start()
            return c

        def wait(r, c):
            copy(r).wait()
            return c

        lax.fori_loop(0, rows_per_step, start, 0)
        lax.fori_loop(0, rows_per_step, wait, 0)


def _gather_rows(h, row_tok, n_used, *, block_rows):
    d = h.shape[1]
    n_rows = row_tok.shape[0]
    rps = _pick_tile(block_rows, 128, V7X_SUBLANES)
    n_steps = n_used * (block_rows // rps)
    kern = functools.partial(_gather_rows_kernel, rows_per_step=rps)
    grid_spec = pltpu.PrefetchScalarGridSpec(
        num_scalar_prefetch=2,
        grid=(n_rows // rps,),
        in_specs=[pl.BlockSpec(memory_space=pl.ANY)],
        out_specs=pl.BlockSpec((rps, d), lambda i, tok, ns: (jnp.minimum(i, ns[0] - 1), 0)),
        scratch_shapes=[pltpu.SemaphoreType.DMA(())],
    )
    return pl.pallas_call(
        kern,
        grid_spec=grid_spec,
        out_shape=jax.ShapeDtypeStruct((n_rows, d), h.dtype),
        compiler_params=_params(("arbitrary",), rps * d * 4),
        name="moe_gather",
    )(row_tok, n_steps, h)


def _experts_kernel(exp_ref, nused_ref, x_ref, wg_ref, wu_ref, wd_ref, o_ref, g_acc, u_acc, h_buf):
    del exp_ref
    s = pl.program_id(1)
    used = pl.program_id(0) < nused_ref[0]

    @pl.when(used & (s < MOE_PHASE_STEPS))
    def _():
        x = _bf16(x_ref[...])
        pg = jnp.dot(x, _bf16(wg_ref[...]), preferred_element_type=jnp.float32)
        pu = jnp.dot(x, _bf16(wu_ref[...]), preferred_element_type=jnp.float32)

        @pl.when(s == 0)
        def _():
            g_acc[...] = pg
            u_acc[...] = pu

        @pl.when(s > 0)
        def _():
            g_acc[...] += pg
            u_acc[...] += pu

    @pl.when(used & (s == MOE_PHASE_STEPS - 1))
    def _():
        g = g_acc[...]
        h_buf[...] = _bf16(g * jax.nn.sigmoid(g) * u_acc[...])

    @pl.when(used & (s >= MOE_PHASE_STEPS))
    def _():
        o_ref[...] = jnp.dot(h_buf[...], _bf16(wd_ref[...]), preferred_element_type=jnp.float32)


def _experts(xs, w_gate, w_up, w_down, layer, blk_exp, n_used, *, block_rows):
    n_rows, d = xs.shape
    d_e = w_gate.shape[3]
    n_blocks = n_rows // block_rows
    p = MOE_PHASE_STEPS
    kc = _exact_div(d, p)
    nc = _exact_div(d, p)
    last = 2 * p - 1

    def blk(b, nu):
        return jnp.minimum(b, nu[0] - 1)

    def step(b, s, nu):
        return jnp.where(b < nu[0], s, last)

    def x_map(b, s, exp, nu):
        return (blk(b, nu), jnp.minimum(step(b, s, nu), p - 1))

    def up_map(b, s, exp, nu):
        return (layer, exp[b], jnp.minimum(step(b, s, nu), p - 1), 0)

    def down_map(b, s, exp, nu):
        return (layer, exp[b], 0, jnp.maximum(step(b, s, nu) - p, 0))

    def out_map(b, s, exp, nu):
        return (blk(b, nu), jnp.maximum(step(b, s, nu) - p, 0))

    grid_spec = pltpu.PrefetchScalarGridSpec(
        num_scalar_prefetch=2,
        grid=(n_blocks, 2 * p),
        in_specs=[pl.BlockSpec((block_rows, kc), x_map),
                  pl.BlockSpec((None, None, kc, d_e), up_map),
                  pl.BlockSpec((None, None, kc, d_e), up_map),
                  pl.BlockSpec((None, None, d_e, nc), down_map)],
        out_specs=pl.BlockSpec((block_rows, nc), out_map),
        scratch_shapes=[pltpu.VMEM((block_rows, d_e), jnp.float32),
                        pltpu.VMEM((block_rows, d_e), jnp.float32),
                        pltpu.VMEM((block_rows, d_e), MXU_DTYPE)],
    )
    blk_bytes = block_rows * kc * 4 + 2 * kc * d_e * 6 + d_e * nc * 6 + block_rows * nc * 4 + block_rows * d_e * 5
    return pl.pallas_call(
        _experts_kernel,
        grid_spec=grid_spec,
        out_shape=jax.ShapeDtypeStruct((n_rows, d), jnp.float32),
        compiler_params=_params(("arbitrary", "arbitrary"), blk_bytes),
        name="moe_experts",
    )(blk_exp, n_used, xs, w_gate, w_up, w_down)


def _combine_kernel(dest_ref, y_ref, x_ref, wt_ref, g_ref, o_ref, ybuf, sem, *, tr, total_rows):
    base = pl.program_id(0) * tr

    def copy(k, r):
        row = dest_ref[k * total_rows + base + r]
        return pltpu.make_async_copy(y_ref.at[pl.ds(row, 1)], ybuf.at[k, pl.ds(r, 1)], sem)

    for k in range(TOP_K):
        def start(r, c, k=k):
            copy(k, r).start()
            return c

        lax.fori_loop(0, tr, start, 0)
    for k in range(TOP_K):
        def wait(r, c, k=k):
            copy(k, r).wait()
            return c

        lax.fori_loop(0, tr, wait, 0)

    for r in range(0, tr, V7X_SUBLANES):
        gi = r // V7X_SUBLANES
        rs = slice(r, r + V7X_SUBLANES)
        y = wt_ref[rs, 0:1] * ybuf[0, rs, :] + wt_ref[rs, 1:2] * ybuf[1, rs, :]
        o_ref[rs, :] = x_ref[rs, :] + g_ref[gi:gi + 1, :] * y


def _combine(yb, dest, wts, x, modg, *, gate_col):
    rows, d = x.shape
    tr = _pick_tile(rows, 128, V7X_SUBLANES)
    kern = functools.partial(_combine_kernel, tr=tr, total_rows=rows)
    grid_spec = pltpu.PrefetchScalarGridSpec(
        num_scalar_prefetch=1,
        grid=(rows // tr,),
        in_specs=[pl.BlockSpec(memory_space=pl.ANY),
                  pl.BlockSpec((tr, d), lambda i, dst: (i, 0)),
                  pl.BlockSpec((tr, TOP_K), lambda i, dst: (i, 0)),
                  pl.BlockSpec((tr // V7X_SUBLANES, d), lambda i, dst: (i, gate_col))],
        out_specs=pl.BlockSpec((tr, d), lambda i, dst: (i, 0)),
        scratch_shapes=[pltpu.VMEM((TOP_K, tr, d), yb.dtype),
                        pltpu.SemaphoreType.DMA(())],
    )
    return pl.pallas_call(
        kern,
        grid_spec=grid_spec,
        out_shape=jax.ShapeDtypeStruct((rows, d), jnp.float32),
        compiler_params=_params(("arbitrary",), tr * d * 14),
        name="moe_combine",
    )(dest.reshape(-1), yb, x, wts.T, modg)


def kernel(x_prompt, x_sample, cache_k, cache_v, state_conv, c_prompt, c_sample, router_w, router_b, norm1_g, norm2_g, w_ada, b_ada, w_in, q_norm_g, k_norm_g, attn_sink, w_ao, dw_w, dw_b, cn_g, cn_b, w_co, w_out, moe_w_gate, moe_w_up, moe_w_down):
    batch, seq, d = x_prompt.shape
    dec_batch, t_new, _ = x_sample.shape
    depth = w_in.shape[0]
    window = cache_k.shape[2]
    n_kv = cache_k.shape[3]
    d_kv = n_kv * cache_k.shape[4]
    d_q = w_ao.shape[1]
    d_conv = w_co.shape[1]
    n_experts = router_w.shape[1]
    if q_norm_g.shape[1] != HEAD_DIM or cache_k.shape[4] != HEAD_DIM or dw_w.shape[1] != CONV_WIDTH:
        raise ValueError("unsupported head / conv geometry")
    if d_q != Q_PER_KV * d_kv or n_experts % N_GROUPS:
        raise ValueError("unsupported head / expert grouping")
    if seq % window or t_new % V7X_SUBLANES or t_new > window:
        raise ValueError("unsupported sequence geometry")
    prompt_rows = batch * seq
    sample_rows = dec_batch * t_new
    rows = prompt_rows + sample_rows
    ga_off = d_q + 2 * d_kv + 2 * d_conv

    x = jnp.concatenate([x_prompt.reshape(prompt_rows, d), x_sample.reshape(sample_rows, d)], axis=0)

    n_seq = batch + dec_batch
    pad = -n_seq % V7X_SUBLANES
    c_all = jnp.concatenate([c_prompt, c_sample, jnp.zeros((pad, d), jnp.float32)], axis=0)
    mod = _adaln(c_all, w_ada, b_ada)
    modg = jnp.concatenate([jnp.repeat(mod[:, :batch], seq // V7X_SUBLANES, axis=1),
                            jnp.repeat(mod[:, batch:n_seq], t_new // V7X_SUBLANES, axis=1)], axis=1)
    SH1, SC1, G1, SH2, SC2, G2 = range(6)

    kp, vp, up, ks, vs, us = [], [], [], [], [], []
    for l in range(depth):
        mg = modg[l]
        h = _norm_mod(x, norm1_g[l], mg, SH1, SC1)
        proj = _in_proj(h, w_in, l)

        a, k_last = _attn_prompt(proj, attn_sink[l], q_norm_g[l], k_norm_g[l], batch=batch, seq=seq,
                                 total_rows=rows, d_q=d_q, d_kv=d_kv, window=window)
        a, k_new = _attn_sample(a, proj, cache_k[l], cache_v[l], attn_sink[l], q_norm_g[l], k_norm_g[l],
                                prompt_rows=prompt_rows, dec_batch=dec_batch, t_new=t_new,
                                d_q=d_q, d_kv=d_kv, window=window)
        cv, u_last = _conv_prompt(proj, dw_w[l], dw_b[l], cn_g[l], cn_b[l], batch=batch, seq=seq,
                                  total_rows=rows, d_q=d_q, d_kv=d_kv, d_conv=d_conv)
        cv, u_new = _conv_sample(cv, proj, state_conv[l], dw_w[l], dw_b[l], cn_g[l], cn_b[l],
                                 prompt_rows=prompt_rows, dec_batch=dec_batch, t_new=t_new,
                                 d_q=d_q, d_kv=d_kv, d_conv=d_conv)

        z = _merge(a, cv, proj, w_ao, w_co, l, ga_off=ga_off)
        x = _out_proj(z, w_out, x, mg, l, gate_col=G1)

        h2, eid, wts = _norm_router(x, norm2_g[l], mg, router_w, router_b, shift_col=SH2, scale_col=SC2)
        row_tok, dest, blk_exp, n_used = _dispatch_plan(eid, n_experts, MOE_BLOCK_ROWS)
        xs = _gather_rows(h2, row_tok, n_used, block_rows=MOE_BLOCK_ROWS)
        yb = _experts(xs, moe_w_gate, moe_w_up, moe_w_down, l, blk_exp, n_used, block_rows=MOE_BLOCK_ROWS)
        x = _combine(yb, dest, wts, x, mg, gate_col=G2)

        v_cols = slice(d_q + d_kv, d_q + 2 * d_kv)
        v_prompt = proj[:prompt_rows, v_cols].reshape(batch, seq, d_kv)[:, seq - window:]
        v_samp = proj[prompt_rows:, v_cols].reshape(dec_batch, t_new, d_kv)
        kp.append(k_last.reshape(batch, window, n_kv, HEAD_DIM))
        vp.append(v_prompt.reshape(batch, window, n_kv, HEAD_DIM))
        up.append(u_last)
        ks.append(jnp.concatenate([cache_k[l][:, t_new:],
                                   k_new.reshape(dec_batch, t_new, n_kv, HEAD_DIM)], axis=1))
        vs.append(jnp.concatenate([cache_v[l][:, t_new:],
                                   v_samp.reshape(dec_batch, t_new, n_kv, HEAD_DIM)], axis=1))
        us.append(u_new)

    y_prompt = x[:prompt_rows].reshape(batch, seq, d)
    y_sample = x[prompt_rows:].reshape(dec_batch, t_new, d)
    return (y_prompt, y_sample, jnp.stack(kp), jnp.stack(vp), jnp.stack(up),
            jnp.stack(ks), jnp.stack(vs), jnp.stack(us))
```

```python
import functools
import math

import jax
import jax.numpy as jnp
from jax import lax
from jax.experimental import pallas as pl
from jax.experimental.pallas import tpu as pltpu

EPS = 1e-6
HEAD_DIM = 128
Q_PER_KV = 4
CONV_WIDTH = 31
N_GROUPS = 8
TOP_K = 2

V7X_SUBLANES = 8
V7X_LANES = 128
V7X_VMEM_BYTES = 64 * 1024 * 1024
_MIB = 1024 * 1024
_BF16_ROWS = 2 * V7X_SUBLANES
MXU_DTYPE = jnp.bfloat16

CONV_HALO = 32
CONV_LEAD = CONV_HALO - (CONV_WIDTH - 1)
MOE_BLOCK_ROWS = 384
MOE_PHASE_STEPS = 4


def _bf16(x):
    return x.astype(MXU_DTYPE)


def _pick_tile(n, target, align):
    best = None
    for t in range(align, min(n, target) + 1, align):
        if n % t == 0:
            best = t
    if best is None:
        raise ValueError(f"no tile for n={n} target={target} align={align}")
    return best


def _exact_div(a, b):
    q, r = divmod(a, b)
    if r:
        raise ValueError(f"{a} is not a multiple of {b}")
    return q


def _params(semantics, block_bytes):
    need = 2 * block_bytes + 8 * _MIB
    limit = int(min(max(need, 16 * _MIB), V7X_VMEM_BYTES - 8 * _MIB))
    return pltpu.CompilerParams(dimension_semantics=semantics, vmem_limit_bytes=limit)


def _adaln_kernel(c_ref, w_ref, b_ref, o_ref):
    c = c_ref[...]
    a = _bf16(c * jax.nn.sigmoid(c))
    o_ref[...] = jnp.dot(a, _bf16(w_ref[...]), preferred_element_type=jnp.float32) + b_ref[...]


def _adaln(c_all, w_ada, b_ada):
    depth, d, n = w_ada.shape
    rows = c_all.shape[0]
    tn = _pick_tile(n, 1024, V7X_LANES)
    return pl.pallas_call(
        _adaln_kernel,
        grid=(depth, n // tn),
        in_specs=[pl.BlockSpec((rows, d), lambda l, j: (0, 0)),
                  pl.BlockSpec((None, d, tn), lambda l, j: (l, 0, j)),
                  pl.BlockSpec((None, 1, tn), lambda l, j: (l, 0, j))],
        out_specs=pl.BlockSpec((None, rows, tn), lambda l, j: (l, 0, j)),
        out_shape=jax.ShapeDtypeStruct((depth, rows, n), jnp.float32),
        compiler_params=_params(("arbitrary", "arbitrary"), d * tn * 6 + rows * d * 4),
        name="adaln",
    )(c_all, w_ada, b_ada.reshape(depth, 1, n))


def _modulated_norm_rows(x_ref, g_ref, sc_ref, sh_ref, row0, n_rows):
    g = g_ref[...]
    parts = []
    for r in range(row0, row0 + n_rows, V7X_SUBLANES):
        x = x_ref[r:r + V7X_SUBLANES, :]
        gi = r // V7X_SUBLANES
        y = x * lax.rsqrt(jnp.mean(x * x, axis=-1, keepdims=True) + EPS) * g
        parts.append(y * (1.0 + sc_ref[gi:gi + 1, :]) + sh_ref[gi:gi + 1, :])
    return parts[0] if len(parts) == 1 else jnp.concatenate(parts, axis=0)


def _norm_mod_kernel(x_ref, g_ref, sc_ref, sh_ref, o_ref):
    for r in range(0, o_ref.shape[0], _BF16_ROWS):
        o_ref[r:r + _BF16_ROWS, :] = _bf16(_modulated_norm_rows(x_ref, g_ref, sc_ref, sh_ref, r, _BF16_ROWS))


def _norm_mod(x, g, modg, shift_col, scale_col):
    rows, d = x.shape
    tr = _pick_tile(rows, 256, _BF16_ROWS)
    tg = tr // V7X_SUBLANES
    return pl.pallas_call(
        _norm_mod_kernel,
        grid=(rows // tr,),
        in_specs=[pl.BlockSpec((tr, d), lambda i: (i, 0)),
                  pl.BlockSpec((1, d), lambda i: (0, 0)),
                  pl.BlockSpec((tg, d), lambda i: (i, scale_col)),
                  pl.BlockSpec((tg, d), lambda i: (i, shift_col))],
        out_specs=pl.BlockSpec((tr, d), lambda i: (i, 0)),
        out_shape=jax.ShapeDtypeStruct((rows, d), MXU_DTYPE),
        compiler_params=_params(("arbitrary",), tr * d * 8),
        name="norm_mod",
    )(x, g.reshape(1, d), modg, modg)


def _matmul_kernel(x_ref, w_ref, o_ref):
    o_ref[...] = jnp.dot(x_ref[...], _bf16(w_ref[...]), preferred_element_type=jnp.float32)


def _in_proj(h, w_in, layer):
    rows, k = h.shape
    n = w_in.shape[2]
    tm = _pick_tile(rows, 768, _BF16_ROWS)
    tn = _pick_tile(n, 512, V7X_LANES)
    return pl.pallas_call(
        _matmul_kernel,
        grid=(n // tn, rows // tm),
        in_specs=[pl.BlockSpec((tm, k), lambda j, i: (i, 0)),
                  pl.BlockSpec((None, k, tn), lambda j, i: (layer, 0, j))],
        out_specs=pl.BlockSpec((tm, tn), lambda j, i: (i, j)),
        out_shape=jax.ShapeDtypeStruct((rows, n), jnp.float32),
        compiler_params=_params(("arbitrary", "arbitrary"), tm * k * 2 + k * tn * 6 + tm * tn * 4),
        name="in_proj",
    )(h, w_in)


def _head_rms(x, g):
    return x * lax.rsqrt(jnp.mean(x * x, axis=-1, keepdims=True) + EPS) * g


def _window_mask(n_q, window, older_present):
    n_rows = Q_PER_KV * n_q
    row = lax.broadcasted_iota(jnp.int32, (n_rows, 2 * window), 0) % n_q
    col = lax.broadcasted_iota(jnp.int32, (n_rows, 2 * window), 1)
    return (col >= row) & (col <= row + window) & ((col >= window) | older_present)


def _kv_group_attention(q_heads, k_all, v_all, sink_vals, valid, n_q):
    q_all = jnp.concatenate(q_heads, axis=0)
    sink_col = jnp.concatenate([jnp.full((n_q, 1), sv, jnp.float32) for sv in sink_vals], axis=0)
    s = lax.dot_general(_bf16(q_all), _bf16(k_all), (((1,), (1,)), ((), ())),
                        preferred_element_type=jnp.float32) * (1.0 / math.sqrt(HEAD_DIM))
    s = jnp.where(valid, s, -jnp.inf)
    m = jnp.maximum(jnp.max(s, axis=-1, keepdims=True), sink_col)
    p = jnp.exp(s - m)
    denom = jnp.sum(p, axis=-1, keepdims=True) + jnp.exp(sink_col - m)
    return jnp.dot(_bf16(p / denom), _bf16(v_all), preferred_element_type=jnp.float32)


def _attn_prompt_kernel(sink_ref, q_ref, kp_ref, kc_ref, vp_ref, vc_ref, qg_ref, kg_ref, o_ref, kn_ref,
                        *, n_kv, window):
    qg = qg_ref[...]
    kg = kg_ref[...]
    valid = _window_mask(window, window, pl.program_id(1) > 0)
    for h in range(n_kv):
        ksl = slice(h * HEAD_DIM, (h + 1) * HEAD_DIM)
        k_cur = _head_rms(kc_ref[:, ksl], kg)
        kn_ref[:, ksl] = k_cur
        k_all = jnp.concatenate([_head_rms(kp_ref[:, ksl], kg), k_cur], axis=0)
        v_all = jnp.concatenate([vp_ref[:, ksl], vc_ref[:, ksl]], axis=0)
        heads = [h * Q_PER_KV + g for g in range(Q_PER_KV)]
        q_heads = [_head_rms(q_ref[:, hq * HEAD_DIM:(hq + 1) * HEAD_DIM], qg) for hq in heads]
        o = _kv_group_attention(q_heads, k_all, v_all, [sink_ref[hq] for hq in heads], valid, window)
        for g, hq in enumerate(heads):
            o_ref[:, hq * HEAD_DIM:(hq + 1) * HEAD_DIM] = _bf16(o[g * window:(g + 1) * window])


def _attn_prompt(proj, sink, q_g, k_g, *, batch, seq, total_rows, d_q, d_kv, window):
    n_kv = d_kv // HEAD_DIM
    nb = _exact_div(seq, window)
    k_col = _exact_div(d_q, d_kv)
    v_col = k_col + 1
    kern = functools.partial(_attn_prompt_kernel, n_kv=n_kv, window=window)

    def cur(b, i, s):
        return b * nb + i

    def prev(b, i, s):
        return b * nb + jnp.maximum(i - 1, 0)

    grid_spec = pltpu.PrefetchScalarGridSpec(
        num_scalar_prefetch=1,
        grid=(batch, nb),
        in_specs=[pl.BlockSpec((window, d_q), lambda b, i, s: (cur(b, i, s), 0)),
                  pl.BlockSpec((window, d_kv), lambda b, i, s: (prev(b, i, s), k_col)),
                  pl.BlockSpec((window, d_kv), lambda b, i, s: (cur(b, i, s), k_col)),
                  pl.BlockSpec((window, d_kv), lambda b, i, s: (prev(b, i, s), v_col)),
                  pl.BlockSpec((window, d_kv), lambda b, i, s: (cur(b, i, s), v_col)),
                  pl.BlockSpec((1, HEAD_DIM), lambda b, i, s: (0, 0)),
                  pl.BlockSpec((1, HEAD_DIM), lambda b, i, s: (0, 0))],
        out_specs=[pl.BlockSpec((window, d_q), lambda b, i, s: (cur(b, i, s), 0)),
                   pl.BlockSpec((None, window, d_kv), lambda b, i, s: (b, 0, 0))],
    )
    return pl.pallas_call(
        kern,
        grid_spec=grid_spec,
        out_shape=[jax.ShapeDtypeStruct((total_rows, d_q), MXU_DTYPE),
                   jax.ShapeDtypeStruct((batch, window, d_kv), jnp.float32)],
        compiler_params=_params(("arbitrary", "arbitrary"), window * (d_q * 6 + d_kv * 20)),
        name="attn_prompt",
    )(sink, proj, proj, proj, proj, proj, q_g.reshape(1, HEAD_DIM), k_g.reshape(1, HEAD_DIM))


def _attn_sample_kernel(sink_ref, a_in_ref, q_ref, kn_ref, vn_ref, kc_ref, vc_ref, qg_ref, kg_ref,
                        o_ref, knew_ref, *, n_kv, window, t_new, seqs):
    del a_in_ref
    qg = qg_ref[...]
    kg = kg_ref[...]
    valid = _window_mask(t_new, window, True)
    fill = jnp.zeros((window - t_new, HEAD_DIM), jnp.float32)
    for sq in range(seqs):
        rows = slice(sq * t_new, (sq + 1) * t_new)
        for h in range(n_kv):
            ksl = slice(h * HEAD_DIM, (h + 1) * HEAD_DIM)
            k_new = _head_rms(kn_ref[rows, ksl], kg)
            knew_ref[rows, ksl] = k_new
            k_all = jnp.concatenate([kc_ref[sq, :, ksl], k_new, fill], axis=0)
            v_all = jnp.concatenate([vc_ref[sq, :, ksl], vn_ref[rows, ksl], fill], axis=0)
            heads = [h * Q_PER_KV + g for g in range(Q_PER_KV)]
            q_heads = [_head_rms(q_ref[rows, hq * HEAD_DIM:(hq + 1) * HEAD_DIM], qg) for hq in heads]
            o = _kv_group_attention(q_heads, k_all, v_all, [sink_ref[hq] for hq in heads], valid, t_new)
            for g, hq in enumerate(heads):
                o_ref[rows, hq * HEAD_DIM:(hq + 1) * HEAD_DIM] = _bf16(o[g * t_new:(g + 1) * t_new])


def _attn_sample(a, proj, cache_k_l, cache_v_l, sink, q_g, k_g, *, prompt_rows, dec_batch, t_new,
                 d_q, d_kv, window):
    n_kv = d_kv // HEAD_DIM
    seqs = _exact_div(_BF16_ROWS, t_new) if _BF16_ROWS % t_new == 0 and dec_batch % (_BF16_ROWS // t_new) == 0 else 1
    tr = seqs * t_new
    row0 = _exact_div(prompt_rows, tr)
    k_col = _exact_div(d_q, d_kv)
    kern = functools.partial(_attn_sample_kernel, n_kv=n_kv, window=window, t_new=t_new, seqs=seqs)
    kc = cache_k_l.reshape(dec_batch, window, d_kv)
    vc = cache_v_l.reshape(dec_batch, window, d_kv)
    grid_spec = pltpu.PrefetchScalarGridSpec(
        num_scalar_prefetch=1,
        grid=(dec_batch // seqs,),
        in_specs=[pl.BlockSpec(memory_space=pl.ANY),
                  pl.BlockSpec((tr, d_q), lambda i, s: (row0 + i, 0)),
                  pl.BlockSpec((tr, d_kv), lambda i, s: (row0 + i, k_col)),
                  pl.BlockSpec((tr, d_kv), lambda i, s: (row0 + i, k_col + 1)),
                  pl.BlockSpec((seqs, window, d_kv), lambda i, s: (i, 0, 0)),
                  pl.BlockSpec((seqs, window, d_kv), lambda i, s: (i, 0, 0)),
                  pl.BlockSpec((1, HEAD_DIM), lambda i, s: (0, 0)),
                  pl.BlockSpec((1, HEAD_DIM), lambda i, s: (0, 0))],
        out_specs=[pl.BlockSpec((tr, d_q), lambda i, s: (row0 + i, 0)),
                   pl.BlockSpec((tr, d_kv), lambda i, s: (i, 0))],
    )
    return pl.pallas_call(
        kern,
        grid_spec=grid_spec,
        out_shape=[jax.ShapeDtypeStruct(a.shape, a.dtype),
                   jax.ShapeDtypeStruct((dec_batch * t_new, d_kv), jnp.float32)],
        input_output_aliases={1: 0},
        compiler_params=_params(("arbitrary",), tr * (d_q * 6 + d_kv * 12) + seqs * window * d_kv * 8),
        name="attn_sample",
    )(sink, a, proj, proj, proj, kc, vc, q_g.reshape(1, HEAD_DIM), k_g.reshape(1, HEAD_DIM))


def _depthwise_conv(uext_ref, base, n_rows, w_ref, b_ref, conv_ref, conv_base, *, row_chunk, ch_chunk):
    d_conv = conv_ref.shape[-1]
    for r0 in range(0, n_rows, row_chunk):
        for c0 in range(0, d_conv, ch_chunk):
            cs = slice(c0, c0 + ch_chunk)
            acc = jnp.broadcast_to(b_ref[:, cs], (row_chunk, ch_chunk))
            for j in range(CONV_WIDTH):
                start = base + r0 + CONV_LEAD + j
                acc = acc + uext_ref[start:start + row_chunk, cs] * w_ref[j:j + 1, cs]
            conv_ref[conv_base + r0:conv_base + r0 + row_chunk, cs] = acc


def _layer_norm_swish(x, g, b):
    mu = jnp.mean(x, axis=-1, keepdims=True)
    xc = x - mu
    var = jnp.mean(xc * xc, axis=-1, keepdims=True)
    y = xc * lax.rsqrt(var + EPS) * g + b
    return y * jax.nn.sigmoid(y)


def _conv_prompt_kernel(uv_ref, ug_ref, w_ref, b_ref, g_ref, bb_ref, o_ref, st_ref, uext_ref, conv_ref,
                        *, tt):
    @pl.when(pl.program_id(1) == 0)
    def _():
        uext_ref[0:CONV_HALO, :] = jnp.zeros((CONV_HALO, uext_ref.shape[1]), jnp.float32)

    uext_ref[CONV_HALO:CONV_HALO + tt, :] = uv_ref[...] * jax.nn.sigmoid(ug_ref[...])
    _depthwise_conv(uext_ref, 0, tt, w_ref, b_ref, conv_ref, 0, row_chunk=32, ch_chunk=256)
    o_ref[...] = _bf16(_layer_norm_swish(conv_ref[...], g_ref[...], bb_ref[...]))
    st_ref[...] = uext_ref[tt + CONV_LEAD:tt + CONV_HALO, :]
    uext_ref[0:CONV_HALO, :] = uext_ref[tt:tt + CONV_HALO, :]


def _conv_prompt(proj, dw_w, dw_b, cn_g, cn_b, *, batch, seq, total_rows, d_q, d_kv, d_conv):
    tt = _pick_tile(seq, 256, 32)
    nt = seq // tt
    uv_col = _exact_div(d_q + 2 * d_kv, d_conv)
    kern = functools.partial(_conv_prompt_kernel, tt=tt)
    return pl.pallas_call(
        kern,
        grid=(batch, nt),
        in_specs=[pl.BlockSpec((tt, d_conv), lambda b, t: (b * nt + t, uv_col)),
                  pl.BlockSpec((tt, d_conv), lambda b, t: (b * nt + t, uv_col + 1)),
                  pl.BlockSpec((CONV_WIDTH, d_conv), lambda b, t: (0, 0)),
                  pl.BlockSpec((1, d_conv), lambda b, t: (0, 0)),
                  pl.BlockSpec((1, d_conv), lambda b, t: (0, 0)),
                  pl.BlockSpec((1, d_conv), lambda b, t: (0, 0))],
        out_specs=[pl.BlockSpec((tt, d_conv), lambda b, t: (b * nt + t, 0)),
                   pl.BlockSpec((None, CONV_WIDTH - 1, d_conv), lambda b, t: (b, 0, 0))],
        out_shape=[jax.ShapeDtypeStruct((total_rows, d_conv), MXU_DTYPE),
                   jax.ShapeDtypeStruct((batch, CONV_WIDTH - 1, d_conv), jnp.float32)],
        scratch_shapes=[pltpu.VMEM((tt + CONV_HALO, d_conv), jnp.float32),
                        pltpu.VMEM((tt, d_conv), jnp.float32)],
        compiler_params=_params(("arbitrary", "arbitrary"), tt * d_conv * 18),
        name="conv_prompt",
    )(proj, proj, dw_w, dw_b.reshape(1, d_conv), cn_g.reshape(1, d_conv), cn_b.reshape(1, d_conv))


def _conv_sample_kernel(cv_in_ref, uv_ref, ug_ref, st_ref, w_ref, b_ref, g_ref, bb_ref, o_ref, nst_ref,
                        uext_ref, conv_ref, *, t_new, seqs):
    del cv_in_ref
    ext = CONV_HALO + t_new
    for sq in range(seqs):
        base = sq * ext
        rows = slice(sq * t_new, (sq + 1) * t_new)
        uext_ref[base:base + CONV_LEAD, :] = jnp.zeros((CONV_LEAD, uext_ref.shape[1]), jnp.float32)
        uext_ref[base + CONV_LEAD:base + CONV_HALO, :] = st_ref[sq]
        uext_ref[base + CONV_HALO:base + ext, :] = uv_ref[rows, :] * jax.nn.sigmoid(ug_ref[rows, :])
        nst_ref[sq] = uext_ref[base + ext - (CONV_WIDTH - 1):base + ext, :]
        _depthwise_conv(uext_ref, base, t_new, w_ref, b_ref, conv_ref, sq * t_new,
                        row_chunk=t_new, ch_chunk=256)
    o_ref[...] = _bf16(_layer_norm_swish(conv_ref[...], g_ref[...], bb_ref[...]))


def _conv_sample(cv, proj, state_l, dw_w, dw_b, cn_g, cn_b, *, prompt_rows, dec_batch, t_new,
                 d_q, d_kv, d_conv):
    seqs = _exact_div(_BF16_ROWS, t_new) if _BF16_ROWS % t_new == 0 and dec_batch % (_BF16_ROWS // t_new) == 0 else 1
    tr = seqs * t_new
    row0 = _exact_div(prompt_rows, tr)
    uv_col = _exact_div(d_q + 2 * d_kv, d_conv)
    kern = functools.partial(_conv_sample_kernel, t_new=t_new, seqs=seqs)
    return pl.pallas_call(
        kern,
        grid=(dec_batch // seqs,),
        in_specs=[pl.BlockSpec(memory_space=pl.ANY),
                  pl.BlockSpec((tr, d_conv), lambda i: (row0 + i, uv_col)),
                  pl.BlockSpec((tr, d_conv), lambda i: (row0 + i, uv_col + 1)),
                  pl.BlockSpec((seqs, CONV_WIDTH - 1, d_conv), lambda i: (i, 0, 0)),
                  pl.BlockSpec((CONV_WIDTH, d_conv), lambda i: (0, 0)),
                  pl.BlockSpec((1, d_conv), lambda i: (0, 0)),
                  pl.BlockSpec((1, d_conv), lambda i: (0, 0)),
                  pl.BlockSpec((1, d_conv), lambda i: (0, 0))],
        out_specs=[pl.BlockSpec((tr, d_conv), lambda i: (row0 + i, 0)),
                   pl.BlockSpec((seqs, CONV_WIDTH - 1, d_conv), lambda i: (i, 0, 0))],
        out_shape=[jax.ShapeDtypeStruct(cv.shape, cv.dtype),
                   jax.ShapeDtypeStruct((dec_batch, CONV_WIDTH - 1, d_conv), jnp.float32)],
        input_output_aliases={0: 0},
        scratch_shapes=[pltpu.VMEM((seqs * (CONV_HALO + t_new), d_conv), jnp.float32),
                        pltpu.VMEM((tr, d_conv), jnp.float32)],
        compiler_params=_params(("arbitrary",), (tr * 12 + seqs * 64 * 4 + 40 * 4) * d_conv),
        name="conv_sample",
    )(cv, proj, proj, state_l, dw_w, dw_b.reshape(1, d_conv), cn_g.reshape(1, d_conv), cn_b.reshape(1, d_conv))


def _merge_kernel(a_ref, cv_ref, wao_ref, wco_ref, ga_ref, gb_ref, o_ref):
    pa = jnp.dot(a_ref[...], _bf16(wao_ref[...]), preferred_element_type=jnp.float32)
    pc = jnp.dot(cv_ref[...], _bf16(wco_ref[...]), preferred_element_type=jnp.float32)
    o_ref[...] = _bf16(jax.nn.sigmoid(ga_ref[...]) * pa + jax.nn.sigmoid(gb_ref[...]) * pc)


def _merge(a, cv, proj, w_ao, w_co, layer, *, ga_off):
    rows, d_q = a.shape
    d_conv = cv.shape[1]
    d = w_ao.shape[2]
    tm = _pick_tile(rows, 384, _BF16_ROWS)
    tn = _pick_tile(d, 512, V7X_LANES)
    ga_col = _exact_div(ga_off, tn)
    gb_col = ga_col + d // tn
    blk = tm * (d_q + d_conv) * 2 + (d_q + d_conv) * tn * 6 + tm * tn * 10
    return pl.pallas_call(
        _merge_kernel,
        grid=(d // tn, rows // tm),
        in_specs=[pl.BlockSpec((tm, d_q), lambda j, i: (i, 0)),
                  pl.BlockSpec((tm, d_conv), lambda j, i: (i, 0)),
                  pl.BlockSpec((None, d_q, tn), lambda j, i: (layer, 0, j)),
                  pl.BlockSpec((None, d_conv, tn), lambda j, i: (layer, 0, j)),
                  pl.BlockSpec((tm, tn), lambda j, i: (i, ga_col + j)),
                  pl.BlockSpec((tm, tn), lambda j, i: (i, gb_col + j))],
        out_specs=pl.BlockSpec((tm, tn), lambda j, i: (i, j)),
        out_shape=jax.ShapeDtypeStruct((rows, d), MXU_DTYPE),
        compiler_params=_params(("arbitrary", "arbitrary"), blk),
        name="merge",
    )(a, cv, w_ao, w_co, proj, proj)


def _out_proj_kernel(z_ref, w_ref, x_ref, g_ref, o_ref):
    m = jnp.dot(z_ref[...], _bf16(w_ref[...]), preferred_element_type=jnp.float32)
    for r in range(0, m.shape[0], V7X_SUBLANES):
        gi = r // V7X_SUBLANES
        rs = slice(r, r + V7X_SUBLANES)
        o_ref[rs, :] = x_ref[rs, :] + g_ref[gi:gi + 1, :] * m[rs]


def _out_proj(z, w_out, x, modg, layer, *, gate_col):
    rows, d = x.shape
    tm = _pick_tile(rows, 768, _BF16_ROWS)
    tn = _pick_tile(d, 512, V7X_LANES)
    gcol = gate_col * (d // tn)
    blk = tm * d * 2 + d * tn * 6 + tm * tn * 9
    return pl.pallas_call(
        _out_proj_kernel,
        grid=(d // tn, rows // tm),
        in_specs=[pl.BlockSpec((tm, d), lambda j, i: (i, 0)),
                  pl.BlockSpec((None, d, tn), lambda j, i: (layer, 0, j)),
                  pl.BlockSpec((tm, tn), lambda j, i: (i, j)),
                  pl.BlockSpec((tm // V7X_SUBLANES, tn), lambda j, i: (i, gcol + j))],
        out_specs=pl.BlockSpec((tm, tn), lambda j, i: (i, j)),
        out_shape=jax.ShapeDtypeStruct((rows, d), jnp.float32),
        compiler_params=_params(("arbitrary", "arbitrary"), blk),
        name="out_proj",
    )(z, w_out, x, modg)


def _first_argmax(x, axis):
    m = jnp.max(x, axis=axis, keepdims=True)
    idx = lax.broadcasted_iota(jnp.int32, x.shape, axis)
    first = jnp.min(jnp.where(x == m, idx, x.shape[axis]), axis=axis, keepdims=True)
    return m, first


def _router_kernel(x_ref, g_ref, sc_ref, sh_ref, rwt_ref, rb_ref, h_ref, eid_ref, wt_ref, *, n_experts):
    tr = x_ref.shape[0]
    for r in range(0, tr, V7X_SUBLANES):
        h_ref[r:r + V7X_SUBLANES, :] = _modulated_norm_rows(x_ref, g_ref, sc_ref, sh_ref, r, V7X_SUBLANES)
    per_group = n_experts // N_GROUPS
    logits = lax.dot_general(rwt_ref[...], h_ref[...], (((1,), (1,)), ((), ())),
                             precision=lax.Precision.HIGHEST,
                             preferred_element_type=jnp.float32)
    s = jax.nn.sigmoid(logits)
    sel = (s + rb_ref[...]).reshape(N_GROUPS, per_group, tr)
    m1, i1 = _first_argmax(sel, 1)
    e_iota = lax.broadcasted_iota(jnp.int32, sel.shape, 1)
    m2 = jnp.max(jnp.where(e_iota == i1, -jnp.inf, sel), axis=1, keepdims=True)
    _, grp = _first_argmax(m1 + m2, 0)
    g_iota = lax.broadcasted_iota(jnp.int32, sel.shape, 0)
    in_group = jnp.max(jnp.where(g_iota == grp, sel, -jnp.inf), axis=0, keepdims=True)
    _, j1 = _first_argmax(in_group, 1)
    j_iota = lax.broadcasted_iota(jnp.int32, in_group.shape, 1)
    _, j2 = _first_argmax(jnp.where(j_iota == j1, -jnp.inf, in_group), 1)
    e1 = (grp * per_group + j1).reshape(1, tr)
    e2 = (grp * per_group + j2).reshape(1, tr)
    x_iota = lax.broadcasted_iota(jnp.int32, s.shape, 0)
    w1 = jnp.sum(jnp.where(x_iota == e1, s, 0.0), axis=0, keepdims=True)
    w2 = jnp.sum(jnp.where(x_iota == e2, s, 0.0), axis=0, keepdims=True)
    tot = w1 + w2
    eid_ref[...] = jnp.concatenate([e1, e2], axis=0)
    wt_ref[...] = jnp.concatenate([w1 / tot, w2 / tot], axis=0)


def _norm_router(x, g, modg, router_w, router_b, *, shift_col, scale_col):
    rows, d = x.shape
    n_experts = router_w.shape[1]
    tr = _pick_tile(rows, 256, V7X_LANES) if rows % V7X_LANES == 0 else rows
    tg = tr // V7X_SUBLANES
    kern = functools.partial(_router_kernel, n_experts=n_experts)
    return pl.pallas_call(
        kern,
        grid=(rows // tr,),
        in_specs=[pl.BlockSpec((tr, d), lambda i: (i, 0)),
                  pl.BlockSpec((1, d), lambda i: (0, 0)),
                  pl.BlockSpec((tg, d), lambda i: (i, scale_col)),
                  pl.BlockSpec((tg, d), lambda i: (i, shift_col)),
                  pl.BlockSpec((n_experts, d), lambda i: (0, 0)),
                  pl.BlockSpec((n_experts, 1), lambda i: (0, 0))],
        out_specs=[pl.BlockSpec((tr, d), lambda i: (i, 0)),
                   pl.BlockSpec((TOP_K, tr), lambda i: (0, i)),
                   pl.BlockSpec((TOP_K, tr), lambda i: (0, i))],
        out_shape=[jax.ShapeDtypeStruct((rows, d), jnp.float32),
                   jax.ShapeDtypeStruct((TOP_K, rows), jnp.int32),
                   jax.ShapeDtypeStruct((TOP_K, rows), jnp.float32)],
        compiler_params=_params(("arbitrary",), tr * d * 16 + n_experts * d * 4),
        name="norm_router",
    )(x, g.reshape(1, d), modg, modg, router_w.T, router_b.reshape(n_experts, 1))


def _dispatch_plan(eid, n_experts, block_rows):
    top_k, rows = eid.shape
    n_assign = top_k * rows
    n_blocks = -(-(n_assign + n_experts * (block_rows - 1)) // block_rows)
    e_flat = eid.reshape(n_assign)
    onehot = (e_flat[:, None] == jnp.arange(n_experts, dtype=jnp.int32)[None, :]).astype(jnp.int32)
    rank = jnp.sum((jnp.cumsum(onehot, axis=0) - onehot) * onehot, axis=1)
    counts = jnp.sum(onehot, axis=0)
    padded = (counts + block_rows - 1) // block_rows * block_rows
    pend = jnp.cumsum(padded)
    poff = pend - padded
    dest = poff[e_flat] + rank
    tok = jnp.tile(jnp.arange(rows, dtype=jnp.int32), top_k)
    row_tok = jnp.zeros((n_blocks * block_rows,), jnp.int32).at[dest].set(tok)
    n_used = pend[-1] // block_rows
    b_all = jnp.arange(n_blocks, dtype=jnp.int32)
    blk = jnp.minimum(b_all, n_used - 1)
    blk_exp = jnp.minimum(jnp.searchsorted(pend, blk * block_rows, side='right'), n_experts - 1)
    filled = jnp.clip(poff[blk_exp] + counts[blk_exp] - b_all * block_rows, 0, block_rows)
    blk_cnt = jnp.where(b_all < n_used, filled, 0)
    return (row_tok.reshape(n_blocks, 1, block_rows), dest.reshape(top_k, rows).astype(jnp.int32),
            blk_exp.astype(jnp.int32), blk_cnt.astype(jnp.int32), n_used.reshape(1).astype(jnp.int32))


def _experts_kernel(exp_ref, cnt_ref, nused_ref, tok0_ref, tok_next_ref, h_ref, wg_ref, wu_ref, wd_ref,
                    o_ref, x_buf, sems, g_acc, u_acc, h_buf):
    del exp_ref
    b = pl.program_id(0)
    s = pl.program_id(1)
    n_used = nused_ref[0]
    used = b < n_used
    slot = b % 2
    kc = wg_ref.shape[0]

    def row_copy(tok_ref, r, slot_):
        return pltpu.make_async_copy(h_ref.at[pl.ds(tok_ref[0, r], 1)], x_buf.at[slot_, pl.ds(r, 1)],
                                     sems.at[slot_])

    def gather_start(tok_ref, n, slot_):
        def body(r, c):
            row_copy(tok_ref, r, slot_).start()
            return c

        lax.fori_loop(0, n, body, 0)

    def gather_wait(tok_ref, n, slot_):
        def body(r, c):
            row_copy(tok_ref, r, slot_).wait()
            return c

        lax.fori_loop(0, n, body, 0)

    @pl.when((b == 0) & (s == 0))
    def _():
        x_buf[...] = jnp.zeros(x_buf.shape, x_buf.dtype)
        gather_start(tok0_ref, cnt_ref[0], 0)

    @pl.when(used & (s == 0))
    def _():
        gather_wait(tok0_ref, cnt_ref[b], slot)

        @pl.when(b + 1 < n_used)
        def _():
            gather_start(tok_next_ref, cnt_ref[b + 1], 1 - slot)

    @pl.when(used & (s < MOE_PHASE_STEPS))
    def _():
        x = _bf16(x_buf[slot, :, pl.ds(pl.multiple_of(s * kc, kc), kc)])
        pg = jnp.dot(x, _bf16(wg_ref[...]), preferred_element_type=jnp.float32)
        pu = jnp.dot(x, _bf16(wu_ref[...]), preferred_element_type=jnp.float32)

        @pl.when(s == 0)
        def _():
            g_acc[...] = pg
            u_acc[...] = pu

        @pl.when(s > 0)
        def _():
            g_acc[...] += pg
            u_acc[...] += pu

    @pl.when(used & (s == MOE_PHASE_STEPS - 1))
    def _():
        g = g_acc[...]
        h_buf[...] = _bf16(g * jax.nn.sigmoid(g) * u_acc[...])

    @pl.when(used & (s >= MOE_PHASE_STEPS))
    def _():
        o_ref[...] = jnp.dot(h_buf[...], _bf16(wd_ref[...]), preferred_element_type=jnp.float32)


def _experts(h, row_tok, w_gate, w_up, w_down, layer, blk_exp, blk_cnt, n_used):
    d = h.shape[1]
    d_e = w_gate.shape[3]
    n_blocks, _, block_rows = row_tok.shape
    n_rows = n_blocks * block_rows
    p = MOE_PHASE_STEPS
    kc = _exact_div(d, p)
    nc = _exact_div(d, p)
    last = 2 * p - 1

    def blk(b, nu):
        return jnp.minimum(b, nu[0] - 1)

    def step(b, s, nu):
        return jnp.where(b < nu[0], s, last)

    def tok_first_map(b, s, exp, cnt, nu):
        return (0, 0, 0)

    def tok_next_map(b, s, exp, cnt, nu):
        return (jnp.minimum(b + 1, nu[0] - 1), 0, 0)

    def up_map(b, s, exp, cnt, nu):
        return (layer, exp[b], jnp.minimum(step(b, s, nu), p - 1), 0)

    def down_map(b, s, exp, cnt, nu):
        return (layer, exp[b], 0, jnp.maximum(step(b, s, nu) - p, 0))

    def out_map(b, s, exp, cnt, nu):
        return (blk(b, nu), jnp.maximum(step(b, s, nu) - p, 0))

    grid_spec = pltpu.PrefetchScalarGridSpec(
        num_scalar_prefetch=3,
        grid=(n_blocks, 2 * p),
        in_specs=[pl.BlockSpec((None, 1, block_rows), tok_first_map, memory_space=pltpu.SMEM),
                  pl.BlockSpec((None, 1, block_rows), tok_next_map, memory_space=pltpu.SMEM),
                  pl.BlockSpec(memory_space=pl.ANY),
                  pl.BlockSpec((None, None, kc, d_e), up_map),
                  pl.BlockSpec((None, None, kc, d_e), up_map),
                  pl.BlockSpec((None, None, d_e, nc), down_map)],
        out_specs=pl.BlockSpec((block_rows, nc), out_map),
        scratch_shapes=[pltpu.VMEM((2, block_rows, d), jnp.float32),
                        pltpu.SemaphoreType.DMA((2,)),
                        pltpu.VMEM((block_rows, d_e), jnp.float32),
                        pltpu.VMEM((block_rows, d_e), jnp.float32),
                        pltpu.VMEM((block_rows, d_e), MXU_DTYPE)],
    )
    blk_bytes = (block_rows * d * 4 + 2 * kc * d_e * 6 + d_e * nc * 6 + block_rows * nc * 4
                 + block_rows * d_e * 5)
    return pl.pallas_call(
        _experts_kernel,
        grid_spec=grid_spec,
        out_shape=jax.ShapeDtypeStruct((n_rows, d), jnp.float32),
        compiler_params=_params(("arbitrary", "arbitrary"), blk_bytes),
        name="moe_experts",
    )(blk_exp, blk_cnt, n_used, row_tok, row_tok, h, w_gate, w_up, w_down)


def _combine_kernel(dest_ref, y_ref, x_ref, wt_ref, g_ref, o_ref, ybuf, sems, *, tr, total_rows):
    i = pl.program_id(0)
    slot = i % 2

    def copy(step, slot_, k, r):
        row = dest_ref[k * total_rows + step * tr + r]
        return pltpu.make_async_copy(y_ref.at[pl.ds(row, 1)], ybuf.at[slot_, k, pl.ds(r, 1)], sems.at[slot_])

    def gather_start(step, slot_):
        for k in range(TOP_K):
            def body(r, c, k=k):
                copy(step, slot_, k, r).start()
                return c

            lax.fori_loop(0, tr, body, 0)

    @pl.when(i == 0)
    def _():
        gather_start(0, 0)

    @pl.when(i + 1 < pl.num_programs(0))
    def _():
        gather_start(i + 1, 1 - slot)

    for k in range(TOP_K):
        def wait(r, c, k=k):
            copy(i, slot, k, r).wait()
            return c

        lax.fori_loop(0, tr, wait, 0)

    for r in range(0, tr, V7X_SUBLANES):
        gi = r // V7X_SUBLANES
        rs = slice(r, r + V7X_SUBLANES)
        y = wt_ref[rs, 0:1] * ybuf[slot, 0, rs, :] + wt_ref[rs, 1:2] * ybuf[slot, 1, rs, :]
        o_ref[rs, :] = x_ref[rs, :] + g_ref[gi:gi + 1, :] * y


def _combine(yb, dest, wts, x, modg, *, gate_col):
    rows, d = x.shape
    tr = _pick_tile(rows, 128, V7X_SUBLANES)
    kern = functools.partial(_combine_kernel, tr=tr, total_rows=rows)
    grid_spec = pltpu.PrefetchScalarGridSpec(
        num_scalar_prefetch=1,
        grid=(rows // tr,),
        in_specs=[pl.BlockSpec(memory_space=pl.ANY),
                  pl.BlockSpec((tr, d), lambda i, dst: (i, 0)),
                  pl.BlockSpec((tr, TOP_K), lambda i, dst: (i, 0)),
                  pl.BlockSpec((tr // V7X_SUBLANES, d), lambda i, dst: (i, gate_col))],
        out_specs=pl.BlockSpec((tr, d), lambda i, dst: (i, 0)),
        scratch_shapes=[pltpu.VMEM((2, TOP_K, tr, d), yb.dtype),
                        pltpu.SemaphoreType.DMA((2,))],
    )
    return pl.pallas_call(
        kern,
        grid_spec=grid_spec,
        out_shape=jax.ShapeDtypeStruct((rows, d), jnp.float32),
        compiler_params=_params(("arbitrary",), tr * d * 16),
        name="moe_combine",
    )(dest.reshape(-1), yb, x, wts.T, modg)


def kernel(x_prompt, x_sample, cache_k, cache_v, state_conv, c_prompt, c_sample, router_w, router_b, norm1_g, norm2_g, w_ada, b_ada, w_in, q_norm_g, k_norm_g, attn_sink, w_ao, dw_w, dw_b, cn_g, cn_b, w_co, w_out, moe_w_gate, moe_w_up, moe_w_down):
    batch, seq, d = x_prompt.shape
    dec_batch, t_new, _ = x_sample.shape
    depth = w_in.shape[0]
    window = cache_k.shape[2]
    n_kv = cache_k.shape[3]
    d_kv = n_kv * cache_k.shape[4]
    d_q = w_ao.shape[1]
    d_conv = w_co.shape[1]
    n_experts = router_w.shape[1]
    if q_norm_g.shape[1] != HEAD_DIM or cache_k.shape[4] != HEAD_DIM or dw_w.shape[1] != CONV_WIDTH:
        raise ValueError("unsupported head / conv geometry")
    if d_q != Q_PER_KV * d_kv or n_experts % N_GROUPS:
        raise ValueError("unsupported head / expert grouping")
    if seq % window or t_new % V7X_SUBLANES or t_new > window:
        raise ValueError("unsupported sequence geometry")
    prompt_rows = batch * seq
    sample_rows = dec_batch * t_new
    rows = prompt_rows + sample_rows
    ga_off = d_q + 2 * d_kv + 2 * d_conv

    x = jnp.concatenate([x_prompt.reshape(prompt_rows, d), x_sample.reshape(sample_rows, d)], axis=0)

    n_seq = batch + dec_batch
    pad = -n_seq % V7X_SUBLANES
    c_all = jnp.concatenate([c_prompt, c_sample, jnp.zeros((pad, d), jnp.float32)], axis=0)
    mod = _adaln(c_all, w_ada, b_ada)
    modg = jnp.concatenate([jnp.repeat(mod[:, :batch], seq // V7X_SUBLANES, axis=1),
                            jnp.repeat(mod[:, batch:n_seq], t_new // V7X_SUBLANES, axis=1)], axis=1)
    SH1, SC1, G1, SH2, SC2, G2 = range(6)

    kp, vp, up, ks, vs, us = [], [], [], [], [], []
    for l in range(depth):
        mg = modg[l]
        h = _norm_mod(x, norm1_g[l], mg, SH1, SC1)
        proj = _in_proj(h, w_in, l)

        a, k_last = _attn_prompt(proj, attn_sink[l], q_norm_g[l], k_norm_g[l], batch=batch, seq=seq,
                                 total_rows=rows, d_q=d_q, d_kv=d_kv, window=window)
        a, k_new = _attn_sample(a, proj, cache_k[l], cache_v[l], attn_sink[l], q_norm_g[l], k_norm_g[l],
                                prompt_rows=prompt_rows, dec_batch=dec_batch, t_new=t_new,
                                d_q=d_q, d_kv=d_kv, window=window)
        cv, u_last = _conv_prompt(proj, dw_w[l], dw_b[l], cn_g[l], cn_b[l], batch=batch, seq=seq,
                                  total_rows=rows, d_q=d_q, d_kv=d_kv, d_conv=d_conv)
        cv, u_new = _conv_sample(cv, proj, state_conv[l], dw_w[l], dw_b[l], cn_g[l], cn_b[l],
                                 prompt_rows=prompt_rows, dec_batch=dec_batch, t_new=t_new,
                                 d_q=d_q, d_kv=d_kv, d_conv=d_conv)

        z = _merge(a, cv, proj, w_ao, w_co, l, ga_off=ga_off)
        x = _out_proj(z, w_out, x, mg, l, gate_col=G1)

        h2, eid, wts = _norm_router(x, norm2_g[l], mg, router_w, router_b, shift_col=SH2, scale_col=SC2)
        row_tok, dest, blk_exp, blk_cnt, n_used = _dispatch_plan(eid, n_experts, MOE_BLOCK_ROWS)
        yb = _experts(h2, row_tok, moe_w_gate, moe_w_up, moe_w_down, l, blk_exp, blk_cnt, n_used)
        x = _combine(yb, dest, wts, x, mg, gate_col=G2)

        v_cols = slice(d_q + d_kv, d_q + 2 * d_kv)
        v_prompt = proj[:prompt_rows, v_cols].reshape(batch, seq, d_kv)[:, seq - window:]
        v_samp = proj[prompt_rows:, v_cols].reshape(dec_batch, t_new, d_kv)
        kp.append(k_last.reshape(batch, window, n_kv, HEAD_DIM))
        vp.append(v_prompt.reshape(batch, window, n_kv, HEAD_DIM))
        up.append(u_last)
        ks.append(jnp.concatenate([cache_k[l][:, t_new:],
                                   k_new.reshape(dec_batch, t_new, n_kv, HEAD_DIM)], axis=1))
        vs.append(jnp.concatenate([cache_v[l][:, t_new:],
                                   v_samp.reshape(dec_batch, t_new, n_kv, HEAD_DIM)], axis=1))
        us.append(u_new)

    y_prompt = x[:prompt_rows].reshape(batch, seq, d)
    y_sample = x[prompt_rows:].reshape(dec_batch, t_new, d)
    return (y_prompt, y_sample, jnp.stack(kp), jnp.stack(vp), jnp.stack(up),
            jnp.stack(ks), jnp.stack(vs), jnp.stack(us))
```

```python
import functools
import math

import jax
import jax.numpy as jnp
from jax import lax
from jax.experimental import pallas as pl
from jax.experimental.pallas import tpu as pltpu

EPS = 1e-6
HEAD_DIM = 128
Q_PER_KV = 4
CONV_WIDTH = 31
N_GROUPS = 8
TOP_K = 2

V7X_SUBLANES = 8
V7X_LANES = 128
V7X_VMEM_BYTES = 64 * 1024 * 1024
_MIB = 1024 * 1024
_BF16_ROWS = 2 * V7X_SUBLANES
MXU_DTYPE = jnp.bfloat16

CONV_HALO = 32
CONV_LEAD = CONV_HALO - (CONV_WIDTH - 1)
CONV_ROW_CHUNK = 64
CONV_CH_CHUNK = 256
MOE_BLOCK_ROWS = 384
MOE_PHASE_STEPS = 4
MOE_WEIGHT_SLOTS = 4
MOE_WEIGHT_LOOKAHEAD = MOE_WEIGHT_SLOTS - 1


def _bf16(x):
    return x.astype(MXU_DTYPE)


def _pick_tile(n, target, align):
    best = None
    for t in range(align, min(n, target) + 1, align):
        if n % t == 0:
            best = t
    if best is None:
        raise ValueError(f"no tile for n={n} target={target} align={align}")
    return best


def _exact_div(a, b):
    q, r = divmod(a, b)
    if r:
        raise ValueError(f"{a} is not a multiple of {b}")
    return q


def _params(semantics, block_bytes):
    need = 2 * block_bytes + 8 * _MIB
    limit = int(min(max(need, 16 * _MIB), V7X_VMEM_BYTES - 8 * _MIB))
    return pltpu.CompilerParams(dimension_semantics=semantics, vmem_limit_bytes=limit)


def _adaln_kernel(c_ref, w_ref, b_ref, o_ref):
    c = c_ref[...]
    a = _bf16(c * jax.nn.sigmoid(c))
    o_ref[...] = jnp.dot(a, _bf16(w_ref[...]), preferred_element_type=jnp.float32) + b_ref[...]


def _adaln(c_all, w_ada, b_ada):
    depth, d, n = w_ada.shape
    rows = c_all.shape[0]
    tn = _pick_tile(n, 1024, V7X_LANES)
    return pl.pallas_call(
        _adaln_kernel,
        grid=(depth, n // tn),
        in_specs=[pl.BlockSpec((rows, d), lambda l, j: (0, 0)),
                  pl.BlockSpec((None, d, tn), lambda l, j: (l, 0, j)),
                  pl.BlockSpec((None, 1, tn), lambda l, j: (l, 0, j))],
        out_specs=pl.BlockSpec((None, rows, tn), lambda l, j: (l, 0, j)),
        out_shape=jax.ShapeDtypeStruct((depth, rows, n), jnp.float32),
        compiler_params=_params(("arbitrary", "arbitrary"), d * tn * 6 + rows * d * 4),
        name="adaln",
    )(c_all, w_ada, b_ada.reshape(depth, 1, n))


def _modulated_norm_rows(x_ref, g_ref, sc_ref, sh_ref, row0, n_rows):
    g = g_ref[...]
    parts = []
    for r in range(row0, row0 + n_rows, V7X_SUBLANES):
        x = x_ref[r:r + V7X_SUBLANES, :]
        gi = r // V7X_SUBLANES
        y = x * lax.rsqrt(jnp.mean(x * x, axis=-1, keepdims=True) + EPS) * g
        parts.append(y * (1.0 + sc_ref[gi:gi + 1, :]) + sh_ref[gi:gi + 1, :])
    return parts[0] if len(parts) == 1 else jnp.concatenate(parts, axis=0)


def _norm_mod_kernel(x_ref, g_ref, sc_ref, sh_ref, o_ref):
    for r in range(0, o_ref.shape[0], _BF16_ROWS):
        o_ref[r:r + _BF16_ROWS, :] = _bf16(_modulated_norm_rows(x_ref, g_ref, sc_ref, sh_ref, r, _BF16_ROWS))


def _norm_mod(x, g, modg, shift_col, scale_col):
    rows, d = x.shape
    tr = _pick_tile(rows, 256, _BF16_ROWS)
    tg = tr // V7X_SUBLANES
    return pl.pallas_call(
        _norm_mod_kernel,
        grid=(rows // tr,),
        in_specs=[pl.BlockSpec((tr, d), lambda i: (i, 0)),
                  pl.BlockSpec((1, d), lambda i: (0, 0)),
                  pl.BlockSpec((tg, d), lambda i: (i, scale_col)),
                  pl.BlockSpec((tg, d), lambda i: (i, shift_col))],
        out_specs=pl.BlockSpec((tr, d), lambda i: (i, 0)),
        out_shape=jax.ShapeDtypeStruct((rows, d), MXU_DTYPE),
        compiler_params=_params(("arbitrary",), tr * d * 8),
        name="norm_mod",
    )(x, g.reshape(1, d), modg, modg)


def _norm_mod_stack_kernel(xp_ref, xs_ref, g_ref, sc_ref, sh_ref, o_ref, x_ref, *, prompt_steps):
    def emit(src_ref):
        x_ref[...] = src_ref[...]
        _norm_mod_kernel(src_ref, g_ref, sc_ref, sh_ref, o_ref)

    is_prompt = pl.program_id(0) < prompt_steps
    pl.when(is_prompt)(lambda: emit(xp_ref))
    pl.when(jnp.logical_not(is_prompt))(lambda: emit(xs_ref))


def _norm_mod_stack(x_prompt, x_sample, g, modg, shift_col, scale_col):
    p_rows, d = x_prompt.shape
    s_rows = x_sample.shape[0]
    rows = p_rows + s_rows
    tr = _pick_tile(math.gcd(p_rows, s_rows), 256, _BF16_ROWS)
    tg = tr // V7X_SUBLANES
    p_steps = p_rows // tr
    kern = functools.partial(_norm_mod_stack_kernel, prompt_steps=p_steps)
    return pl.pallas_call(
        kern,
        grid=(rows // tr,),
        in_specs=[pl.BlockSpec((tr, d), lambda i: (jnp.minimum(i, p_steps - 1), 0)),
                  pl.BlockSpec((tr, d), lambda i: (jnp.maximum(i - p_steps, 0), 0)),
                  pl.BlockSpec((1, d), lambda i: (0, 0)),
                  pl.BlockSpec((tg, d), lambda i: (i, scale_col)),
                  pl.BlockSpec((tg, d), lambda i: (i, shift_col))],
        out_specs=[pl.BlockSpec((tr, d), lambda i: (i, 0)),
                   pl.BlockSpec((tr, d), lambda i: (i, 0))],
        out_shape=[jax.ShapeDtypeStruct((rows, d), MXU_DTYPE),
                   jax.ShapeDtypeStruct((rows, d), jnp.float32)],
        compiler_params=_params(("arbitrary",), tr * d * 16),
        name="norm_mod_stack",
    )(x_prompt, x_sample, g.reshape(1, d), modg, modg)


def _matmul_kernel(x_ref, w_ref, o_ref):
    o_ref[...] = jnp.dot(x_ref[...], _bf16(w_ref[...]), preferred_element_type=jnp.float32)


def _in_proj(h, w_in, layer):
    rows, k = h.shape
    n = w_in.shape[2]
    tm = _pick_tile(rows, 1408, _BF16_ROWS)
    tn = _pick_tile(n, 512, V7X_LANES)
    return pl.pallas_call(
        _matmul_kernel,
        grid=(n // tn, rows // tm),
        in_specs=[pl.BlockSpec((tm, k), lambda j, i: (i, 0)),
                  pl.BlockSpec((None, k, tn), lambda j, i: (layer, 0, j))],
        out_specs=pl.BlockSpec((tm, tn), lambda j, i: (i, j)),
        out_shape=jax.ShapeDtypeStruct((rows, n), jnp.float32),
        compiler_params=_params(("arbitrary", "arbitrary"), tm * k * 2 + k * tn * 6 + tm * tn * 4),
        name="in_proj",
    )(h, w_in)


def _head_rms(x, g):
    return x * lax.rsqrt(jnp.mean(x * x, axis=-1, keepdims=True) + EPS) * g


def _window_mask(n_q, window, older_present):
    n_rows = Q_PER_KV * n_q
    row = lax.broadcasted_iota(jnp.int32, (n_rows, 2 * window), 0) % n_q
    col = lax.broadcasted_iota(jnp.int32, (n_rows, 2 * window), 1)
    return (col >= row) & (col <= row + window) & ((col >= window) | older_present)


def _kv_group_attention(q_heads, k_all, v_all, sink_vals, valid, n_q):
    q_all = jnp.concatenate(q_heads, axis=0)
    sink_col = jnp.concatenate([jnp.full((n_q, 1), sv, jnp.float32) for sv in sink_vals], axis=0)
    s = lax.dot_general(_bf16(q_all), _bf16(k_all), (((1,), (1,)), ((), ())),
                        preferred_element_type=jnp.float32) * (1.0 / math.sqrt(HEAD_DIM))
    s = jnp.where(valid, s, -jnp.inf)
    m = jnp.maximum(jnp.max(s, axis=-1, keepdims=True), sink_col)
    p = jnp.exp(s - m)
    denom = jnp.sum(p, axis=-1, keepdims=True) + jnp.exp(sink_col - m)
    return jnp.dot(_bf16(p / denom), _bf16(v_all), preferred_element_type=jnp.float32)


def _attn_prompt_kernel(sink_ref, q_ref, kp_ref, kc_ref, vp_ref, vc_ref, qg_ref, kg_ref, o_ref, kn_ref,
                        *, n_kv, window):
    qg = qg_ref[...]
    kg = kg_ref[...]
    valid = _window_mask(window, window, pl.program_id(1) > 0)
    for h in range(n_kv):
        ksl = slice(h * HEAD_DIM, (h + 1) * HEAD_DIM)
        k_cur = _head_rms(kc_ref[:, ksl], kg)
        kn_ref[:, ksl] = k_cur
        k_all = jnp.concatenate([_head_rms(kp_ref[:, ksl], kg), k_cur], axis=0)
        v_all = jnp.concatenate([vp_ref[:, ksl], vc_ref[:, ksl]], axis=0)
        heads = [h * Q_PER_KV + g for g in range(Q_PER_KV)]
        q_heads = [_head_rms(q_ref[:, hq * HEAD_DIM:(hq + 1) * HEAD_DIM], qg) for hq in heads]
        o = _kv_group_attention(q_heads, k_all, v_all, [sink_ref[hq] for hq in heads], valid, window)
        for g, hq in enumerate(heads):
            o_ref[:, hq * HEAD_DIM:(hq + 1) * HEAD_DIM] = _bf16(o[g * window:(g + 1) * window])


def _attn_prompt(proj, sink, q_g, k_g, *, batch, seq, total_rows, d_q, d_kv, window):
    n_kv = d_kv // HEAD_DIM
    nb = _exact_div(seq, window)
    k_col = _exact_div(d_q, d_kv)
    v_col = k_col + 1
    kern = functools.partial(_attn_prompt_kernel, n_kv=n_kv, window=window)

    def cur(b, i, s):
        return b * nb + i

    def prev(b, i, s):
        return b * nb + jnp.maximum(i - 1, 0)

    grid_spec = pltpu.PrefetchScalarGridSpec(
        num_scalar_prefetch=1,
        grid=(batch, nb),
        in_specs=[pl.BlockSpec((window, d_q), lambda b, i, s: (cur(b, i, s), 0)),
                  pl.BlockSpec((window, d_kv), lambda b, i, s: (prev(b, i, s), k_col)),
                  pl.BlockSpec((window, d_kv), lambda b, i, s: (cur(b, i, s), k_col)),
                  pl.BlockSpec((window, d_kv), lambda b, i, s: (prev(b, i, s), v_col)),
                  pl.BlockSpec((window, d_kv), lambda b, i, s: (cur(b, i, s), v_col)),
                  pl.BlockSpec((1, HEAD_DIM), lambda b, i, s: (0, 0)),
                  pl.BlockSpec((1, HEAD_DIM), lambda b, i, s: (0, 0))],
        out_specs=[pl.BlockSpec((window, d_q), lambda b, i, s: (cur(b, i, s), 0)),
                   pl.BlockSpec((None, window, d_kv), lambda b, i, s: (b, 0, 0))],
    )
    return pl.pallas_call(
        kern,
        grid_spec=grid_spec,
        out_shape=[jax.ShapeDtypeStruct((total_rows, d_q), MXU_DTYPE),
                   jax.ShapeDtypeStruct((batch, window, d_kv), jnp.float32)],
        compiler_params=_params(("arbitrary", "arbitrary"), window * (d_q * 6 + d_kv * 20)),
        name="attn_prompt",
    )(sink, proj, proj, proj, proj, proj, q_g.reshape(1, HEAD_DIM), k_g.reshape(1, HEAD_DIM))


def _attn_sample_kernel(sink_ref, a_in_ref, q_ref, kn_ref, vn_ref, kc_ref, vc_ref, qg_ref, kg_ref,
                        o_ref, knew_ref, *, n_kv, window, t_new, seqs):
    del a_in_ref
    qg = qg_ref[...]
    kg = kg_ref[...]
    valid = _window_mask(t_new, window, True)
    fill = jnp.zeros((window - t_new, HEAD_DIM), jnp.float32)
    for sq in range(seqs):
        rows = slice(sq * t_new, (sq + 1) * t_new)
        for h in range(n_kv):
            ksl = slice(h * HEAD_DIM, (h + 1) * HEAD_DIM)
            k_new = _head_rms(kn_ref[rows, ksl], kg)
            knew_ref[rows, ksl] = k_new
            k_all = jnp.concatenate([kc_ref[sq, :, ksl], k_new, fill], axis=0)
            v_all = jnp.concatenate([vc_ref[sq, :, ksl], vn_ref[rows, ksl], fill], axis=0)
            heads = [h * Q_PER_KV + g for g in range(Q_PER_KV)]
            q_heads = [_head_rms(q_ref[rows, hq * HEAD_DIM:(hq + 1) * HEAD_DIM], qg) for hq in heads]
            o = _kv_group_attention(q_heads, k_all, v_all, [sink_ref[hq] for hq in heads], valid, t_new)
            for g, hq in enumerate(heads):
                o_ref[rows, hq * HEAD_DIM:(hq + 1) * HEAD_DIM] = _bf16(o[g * t_new:(g + 1) * t_new])


def _attn_sample(a, proj, cache_k_l, cache_v_l, sink, q_g, k_g, *, prompt_rows, dec_batch, t_new,
                 d_q, d_kv, window):
    n_kv = d_kv // HEAD_DIM
    seqs = _exact_div(_BF16_ROWS, t_new) if _BF16_ROWS % t_new == 0 and dec_batch % (_BF16_ROWS // t_new) == 0 else 1
    tr = seqs * t_new
    row0 = _exact_div(prompt_rows, tr)
    k_col = _exact_div(d_q, d_kv)
    kern = functools.partial(_attn_sample_kernel, n_kv=n_kv, window=window, t_new=t_new, seqs=seqs)
    kc = cache_k_l.reshape(dec_batch, window, d_kv)
    vc = cache_v_l.reshape(dec_batch, window, d_kv)
    grid_spec = pltpu.PrefetchScalarGridSpec(
        num_scalar_prefetch=1,
        grid=(dec_batch // seqs,),
        in_specs=[pl.BlockSpec(memory_space=pl.ANY),
                  pl.BlockSpec((tr, d_q), lambda i, s: (row0 + i, 0)),
                  pl.BlockSpec((tr, d_kv), lambda i, s: (row0 + i, k_col)),
                  pl.BlockSpec((tr, d_kv), lambda i, s: (row0 + i, k_col + 1)),
                  pl.BlockSpec((seqs, window, d_kv), lambda i, s: (i, 0, 0)),
                  pl.BlockSpec((seqs, window, d_kv), lambda i, s: (i, 0, 0)),
                  pl.BlockSpec((1, HEAD_DIM), lambda i, s: (0, 0)),
                  pl.BlockSpec((1, HEAD_DIM), lambda i, s: (0, 0))],
        out_specs=[pl.BlockSpec((tr, d_q), lambda i, s: (row0 + i, 0)),
                   pl.BlockSpec((tr, d_kv), lambda i, s: (i, 0))],
    )
    return pl.pallas_call(
        kern,
        grid_spec=grid_spec,
        out_shape=[jax.ShapeDtypeStruct(a.shape, a.dtype),
                   jax.ShapeDtypeStruct((dec_batch * t_new, d_kv), jnp.float32)],
        input_output_aliases={1: 0},
        compiler_params=_params(("arbitrary",), tr * (d_q * 6 + d_kv * 12) + seqs * window * d_kv * 8),
        name="attn_sample",
    )(sink, a, proj, proj, proj, kc, vc, q_g.reshape(1, HEAD_DIM), k_g.reshape(1, HEAD_DIM))


def _depthwise_conv(uext_ref, base, n_rows, w_ref, b_ref, conv_ref, conv_base, part_ref):
    d_conv = conv_ref.shape[-1]
    ch_chunk = part_ref.shape[-1]
    for c0 in range(0, d_conv, ch_chunk):
        cs = slice(c0, c0 + ch_chunk)
        for s in range(V7X_SUBLANES):
            span = n_rows + -(-(CONV_LEAD + s) // V7X_SUBLANES) * V7X_SUBLANES
            for t0 in range(0, span, CONV_ROW_CHUNK):
                rc = min(CONV_ROW_CHUNK, span - t0)
                acc = None
                for j in range(s, CONV_WIDTH, V7X_SUBLANES):
                    start = base + t0 + j - s
                    term = uext_ref[start:start + rc, cs] * w_ref[j:j + 1, cs]
                    acc = term if acc is None else acc + term
                part_ref[s, t0:t0 + rc, :] = acc
        for r0 in range(0, n_rows, CONV_ROW_CHUNK):
            rc = min(CONV_ROW_CHUNK, n_rows - r0)
            acc = jnp.broadcast_to(b_ref[:, cs], (rc, ch_chunk))
            for s in range(V7X_SUBLANES):
                off = r0 + CONV_LEAD + s
                acc = acc + part_ref[s, off:off + rc, :]
            conv_ref[conv_base + r0:conv_base + r0 + rc, cs] = acc


def _conv_ch_chunk(d_conv):
    return _pick_tile(d_conv, CONV_CH_CHUNK, V7X_LANES)


def _layer_norm_swish(x, g, b):
    mu = jnp.mean(x, axis=-1, keepdims=True)
    xc = x - mu
    var = jnp.mean(xc * xc, axis=-1, keepdims=True)
    y = xc * lax.rsqrt(var + EPS) * g + b
    return y * jax.nn.sigmoid(y)


def _conv_prompt_kernel(uv_ref, ug_ref, w_ref, b_ref, g_ref, bb_ref, o_ref, st_ref, uext_ref, conv_ref,
                        part_ref, *, tt):
    @pl.when(pl.program_id(1) == 0)
    def _():
        uext_ref[0:CONV_HALO, :] = jnp.zeros((CONV_HALO, uext_ref.shape[1]), jnp.float32)

    uext_ref[CONV_HALO:CONV_HALO + tt, :] = uv_ref[...] * jax.nn.sigmoid(ug_ref[...])
    _depthwise_conv(uext_ref, 0, tt, w_ref, b_ref, conv_ref, 0, part_ref)
    o_ref[...] = _bf16(_layer_norm_swish(conv_ref[...], g_ref[...], bb_ref[...]))
    st_ref[...] = uext_ref[tt + CONV_LEAD:tt + CONV_HALO, :]
    uext_ref[0:CONV_HALO, :] = uext_ref[tt:tt + CONV_HALO, :]


def _conv_prompt(proj, dw_w, dw_b, cn_g, cn_b, *, batch, seq, total_rows, d_q, d_kv, d_conv):
    tt = _pick_tile(seq, 256, 32)
    nt = seq // tt
    uv_col = _exact_div(d_q + 2 * d_kv, d_conv)
    kern = functools.partial(_conv_prompt_kernel, tt=tt)
    return pl.pallas_call(
        kern,
        grid=(batch, nt),
        in_specs=[pl.BlockSpec((tt, d_conv), lambda b, t: (b * nt + t, uv_col)),
                  pl.BlockSpec((tt, d_conv), lambda b, t: (b * nt + t, uv_col + 1)),
                  pl.BlockSpec((CONV_WIDTH, d_conv), lambda b, t: (0, 0)),
                  pl.BlockSpec((1, d_conv), lambda b, t: (0, 0)),
                  pl.BlockSpec((1, d_conv), lambda b, t: (0, 0)),
                  pl.BlockSpec((1, d_conv), lambda b, t: (0, 0))],
        out_specs=[pl.BlockSpec((tt, d_conv), lambda b, t: (b * nt + t, 0)),
                   pl.BlockSpec((None, CONV_WIDTH - 1, d_conv), lambda b, t: (b, 0, 0))],
        out_shape=[jax.ShapeDtypeStruct((total_rows, d_conv), MXU_DTYPE),
                   jax.ShapeDtypeStruct((batch, CONV_WIDTH - 1, d_conv), jnp.float32)],
        scratch_shapes=[pltpu.VMEM((tt + CONV_HALO, d_conv), jnp.float32),
                        pltpu.VMEM((tt, d_conv), jnp.float32),
                        pltpu.VMEM((V7X_SUBLANES, tt + 2 * V7X_SUBLANES, _conv_ch_chunk(d_conv)), jnp.float32)],
        compiler_params=_params(("arbitrary", "arbitrary"), tt * d_conv * 18),
        name="conv_prompt",
    )(proj, proj, dw_w, dw_b.reshape(1, d_conv), cn_g.reshape(1, d_conv), cn_b.reshape(1, d_conv))


def _conv_sample_kernel(cv_in_ref, uv_ref, ug_ref, st_ref, w_ref, b_ref, g_ref, bb_ref, o_ref, nst_ref,
                        uext_ref, conv_ref, part_ref, *, t_new, seqs):
    del cv_in_ref
    ext = CONV_HALO + t_new
    for sq in range(seqs):
        base = sq * ext
        rows = slice(sq * t_new, (sq + 1) * t_new)
        uext_ref[base:base + CONV_LEAD, :] = jnp.zeros((CONV_LEAD, uext_ref.shape[1]), jnp.float32)
        uext_ref[base + CONV_LEAD:base + CONV_HALO, :] = st_ref[sq]
        uext_ref[base + CONV_HALO:base + ext, :] = uv_ref[rows, :] * jax.nn.sigmoid(ug_ref[rows, :])
        nst_ref[sq] = uext_ref[base + ext - (CONV_WIDTH - 1):base + ext, :]
        _depthwise_conv(uext_ref, base, t_new, w_ref, b_ref, conv_ref, sq * t_new, part_ref)
    o_ref[...] = _bf16(_layer_norm_swish(conv_ref[...], g_ref[...], bb_ref[...]))


def _conv_sample(cv, proj, state_l, dw_w, dw_b, cn_g, cn_b, *, prompt_rows, dec_batch, t_new,
                 d_q, d_kv, d_conv):
    seqs = _exact_div(_BF16_ROWS, t_new) if _BF16_ROWS % t_new == 0 and dec_batch % (_BF16_ROWS // t_new) == 0 else 1
    tr = seqs * t_new
    row0 = _exact_div(prompt_rows, tr)
    uv_col = _exact_div(d_q + 2 * d_kv, d_conv)
    kern = functools.partial(_conv_sample_kernel, t_new=t_new, seqs=seqs)
    return pl.pallas_call(
        kern,
        grid=(dec_batch // seqs,),
        in_specs=[pl.BlockSpec(memory_space=pl.ANY),
                  pl.BlockSpec((tr, d_conv), lambda i: (row0 + i, uv_col)),
                  pl.BlockSpec((tr, d_conv), lambda i: (row0 + i, uv_col + 1)),
                  pl.BlockSpec((seqs, CONV_WIDTH - 1, d_conv), lambda i: (i, 0, 0)),
                  pl.BlockSpec((CONV_WIDTH, d_conv), lambda i: (0, 0)),
                  pl.BlockSpec((1, d_conv), lambda i: (0, 0)),
                  pl.BlockSpec((1, d_conv), lambda i: (0, 0)),
                  pl.BlockSpec((1, d_conv), lambda i: (0, 0))],
        out_specs=[pl.BlockSpec((tr, d_conv), lambda i: (row0 + i, 0)),
                   pl.BlockSpec((seqs, CONV_WIDTH - 1, d_conv), lambda i: (i, 0, 0))],
        out_shape=[jax.ShapeDtypeStruct(cv.shape, cv.dtype),
                   jax.ShapeDtypeStruct((dec_batch, CONV_WIDTH - 1, d_conv), jnp.float32)],
        input_output_aliases={0: 0},
        scratch_shapes=[pltpu.VMEM((seqs * (CONV_HALO + t_new), d_conv), jnp.float32),
                        pltpu.VMEM((tr, d_conv), jnp.float32),
                        pltpu.VMEM((V7X_SUBLANES, t_new + 2 * V7X_SUBLANES, _conv_ch_chunk(d_conv)), jnp.float32)],
        compiler_params=_params(("arbitrary",), (tr * 12 + seqs * 64 * 4 + 40 * 4) * d_conv),
        name="conv_sample",
    )(cv, proj, proj, state_l, dw_w, dw_b.reshape(1, d_conv), cn_g.reshape(1, d_conv), cn_b.reshape(1, d_conv))


def _merge_kernel(a_ref, cv_ref, wao_ref, wco_ref, ga_ref, gb_ref, o_ref):
    pa = jnp.dot(a_ref[...], _bf16(wao_ref[...]), preferred_element_type=jnp.float32)
    pc = jnp.dot(cv_ref[...], _bf16(wco_ref[...]), preferred_element_type=jnp.float32)
    o_ref[...] = _bf16(jax.nn.sigmoid(ga_ref[...]) * pa + jax.nn.sigmoid(gb_ref[...]) * pc)


def _merge(a, cv, proj, w_ao, w_co, layer, *, ga_off):
    rows, d_q = a.shape
    d_conv = cv.shape[1]
    d = w_ao.shape[2]
    tm = _pick_tile(rows, 384, _BF16_ROWS)
    tn = _pick_tile(d, 512, V7X_LANES)
    ga_col = _exact_div(ga_off, tn)
    gb_col = ga_col + d // tn
    blk = tm * (d_q + d_conv) * 2 + (d_q + d_conv) * tn * 6 + tm * tn * 10
    return pl.pallas_call(
        _merge_kernel,
        grid=(d // tn, rows // tm),
        in_specs=[pl.BlockSpec((tm, d_q), lambda j, i: (i, 0)),
                  pl.BlockSpec((tm, d_conv), lambda j, i: (i, 0)),
                  pl.BlockSpec((None, d_q, tn), lambda j, i: (layer, 0, j)),
                  pl.BlockSpec((None, d_conv, tn), lambda j, i: (layer, 0, j)),
                  pl.BlockSpec((tm, tn), lambda j, i: (i, ga_col + j)),
                  pl.BlockSpec((tm, tn), lambda j, i: (i, gb_col + j))],
        out_specs=pl.BlockSpec((tm, tn), lambda j, i: (i, j)),
        out_shape=jax.ShapeDtypeStruct((rows, d), MXU_DTYPE),
        compiler_params=_params(("arbitrary", "arbitrary"), blk),
        name="merge",
    )(a, cv, w_ao, w_co, proj, proj)


def _out_proj_kernel(z_ref, w_ref, x_ref, g_ref, o_ref):
    m = jnp.dot(z_ref[...], _bf16(w_ref[...]), preferred_element_type=jnp.float32)
    for r in range(0, m.shape[0], V7X_SUBLANES):
        gi = r // V7X_SUBLANES
        rs = slice(r, r + V7X_SUBLANES)
        o_ref[rs, :] = x_ref[rs, :] + g_ref[gi:gi + 1, :] * m[rs]


def _out_proj(z, w_out, x, modg, layer, *, gate_col):
    rows, d = x.shape
    tm = _pick_tile(rows, 768, _BF16_ROWS)
    tn = _pick_tile(d, 512, V7X_LANES)
    gcol = gate_col * (d // tn)
    blk = tm * d * 2 + d * tn * 6 + tm * tn * 9
    return pl.pallas_call(
        _out_proj_kernel,
        grid=(d // tn, rows // tm),
        in_specs=[pl.BlockSpec((tm, d), lambda j, i: (i, 0)),
                  pl.BlockSpec((None, d, tn), lambda j, i: (layer, 0, j)),
                  pl.BlockSpec((tm, tn), lambda j, i: (i, j)),
                  pl.BlockSpec((tm // V7X_SUBLANES, tn), lambda j, i: (i, gcol + j))],
        out_specs=pl.BlockSpec((tm, tn), lambda j, i: (i, j)),
        out_shape=jax.ShapeDtypeStruct((rows, d), jnp.float32),
        compiler_params=_params(("arbitrary", "arbitrary"), blk),
        name="out_proj",
    )(z, w_out, x, modg)


def _first_argmax(x, axis):
    m = jnp.max(x, axis=axis, keepdims=True)
    idx = lax.broadcasted_iota(jnp.int32, x.shape, axis)
    first = jnp.min(jnp.where(x == m, idx, x.shape[axis]), axis=axis, keepdims=True)
    return m, first


def _router_kernel(x_ref, g_ref, sc_ref, sh_ref, rwt_ref, rb_ref, h_ref, eid_ref, wt_ref, *, n_experts):
    tr = x_ref.shape[0]
    for r in range(0, tr, V7X_SUBLANES):
        h_ref[r:r + V7X_SUBLANES, :] = _modulated_norm_rows(x_ref, g_ref, sc_ref, sh_ref, r, V7X_SUBLANES)
    per_group = n_experts // N_GROUPS
    logits = lax.dot_general(rwt_ref[...], h_ref[...], (((1,), (1,)), ((), ())),
                             precision=lax.Precision.HIGHEST,
                             preferred_element_type=jnp.float32)
    s = jax.nn.sigmoid(logits)
    sel = (s + rb_ref[...]).reshape(N_GROUPS, per_group, tr)
    m1, i1 = _first_argmax(sel, 1)
    e_iota = lax.broadcasted_iota(jnp.int32, sel.shape, 1)
    m2 = jnp.max(jnp.where(e_iota == i1, -jnp.inf, sel), axis=1, keepdims=True)
    _, grp = _first_argmax(m1 + m2, 0)
    g_iota = lax.broadcasted_iota(jnp.int32, sel.shape, 0)
    in_group = jnp.max(jnp.where(g_iota == grp, sel, -jnp.inf), axis=0, keepdims=True)
    _, j1 = _first_argmax(in_group, 1)
    j_iota = lax.broadcasted_iota(jnp.int32, in_group.shape, 1)
    _, j2 = _first_argmax(jnp.where(j_iota == j1, -jnp.inf, in_group), 1)
    e1 = (grp * per_group + j1).reshape(1, tr)
    e2 = (grp * per_group + j2).reshape(1, tr)
    x_iota = lax.broadcasted_iota(jnp.int32, s.shape, 0)
    w1 = jnp.sum(jnp.where(x_iota == e1, s, 0.0), axis=0, keepdims=True)
    w2 = jnp.sum(jnp.where(x_iota == e2, s, 0.0), axis=0, keepdims=True)
    tot = w1 + w2
    eid_ref[...] = jnp.concatenate([e1, e2], axis=0)
    wt_ref[...] = jnp.concatenate([w1 / tot, w2 / tot], axis=0)


def _norm_router(x, g, modg, router_w, router_b, *, shift_col, scale_col):
    rows, d = x.shape
    n_experts = router_w.shape[1]
    tr = _pick_tile(rows, 256, V7X_LANES) if rows % V7X_LANES == 0 else rows
    tg = tr // V7X_SUBLANES
    kern = functools.partial(_router_kernel, n_experts=n_experts)
    return pl.pallas_call(
        kern,
        grid=(rows // tr,),
        in_specs=[pl.BlockSpec((tr, d), lambda i: (i, 0)),
                  pl.BlockSpec((1, d), lambda i: (0, 0)),
                  pl.BlockSpec((tg, d), lambda i: (i, scale_col)),
                  pl.BlockSpec((tg, d), lambda i: (i, shift_col)),
                  pl.BlockSpec((n_experts, d), lambda i: (0, 0)),
                  pl.BlockSpec((n_experts, 1), lambda i: (0, 0))],
        out_specs=[pl.BlockSpec((tr, d), lambda i: (i, 0)),
                   pl.BlockSpec((TOP_K, tr), lambda i: (0, i)),
                   pl.BlockSpec((TOP_K, tr), lambda i: (0, i))],
        out_shape=[jax.ShapeDtypeStruct((rows, d), jnp.float32),
                   jax.ShapeDtypeStruct((TOP_K, rows), jnp.int32),
                   jax.ShapeDtypeStruct((TOP_K, rows), jnp.float32)],
        compiler_params=_params(("arbitrary",), tr * d * 16 + n_experts * d * 4),
        name="norm_router",
    )(x, g.reshape(1, d), modg, modg, router_w.T, router_b.reshape(n_experts, 1))


def _dispatch_plan(eid, n_experts, block_rows):
    top_k, rows = eid.shape
    n_assign = top_k * rows
    n_blocks = -(-(n_assign + n_experts * (block_rows - 1)) // block_rows)
    e_flat = eid.reshape(n_assign)
    onehot = (e_flat[:, None] == jnp.arange(n_experts, dtype=jnp.int32)[None, :]).astype(jnp.int32)
    rank = jnp.sum((jnp.cumsum(onehot, axis=0) - onehot) * onehot, axis=1)
    counts = jnp.sum(onehot, axis=0)
    padded = (counts + block_rows - 1) // block_rows * block_rows
    pend = jnp.cumsum(padded)
    poff = pend - padded
    dest = poff[e_flat] + rank
    tok = jnp.tile(jnp.arange(rows, dtype=jnp.int32), top_k)
    row_tok = jnp.zeros((n_blocks * block_rows,), jnp.int32).at[dest].set(tok)
    n_used = pend[-1] // block_rows
    b_all = jnp.arange(n_blocks, dtype=jnp.int32)
    blk = jnp.minimum(b_all, n_used - 1)
    blk_exp = jnp.minimum(jnp.searchsorted(pend, blk * block_rows, side='right'), n_experts - 1)
    filled = jnp.clip(poff[blk_exp] + counts[blk_exp] - b_all * block_rows, 0, block_rows)
    blk_cnt = jnp.where(b_all < n_used, filled, 0)
    return (row_tok.reshape(n_blocks, 1, block_rows), dest.reshape(top_k, rows).astype(jnp.int32),
            blk_exp.astype(jnp.int32), blk_cnt.astype(jnp.int32), n_used.reshape(1).astype(jnp.int32))


def _experts_kernel(exp_ref, cnt_ref, nused_ref, tok0_ref, tok_next_ref, h_ref, wg_hbm, wu_hbm, wd_hbm,
                    o_ref, x_buf, sems, w_buf, w_sems, g_acc, u_acc, h_buf, *, layer):
    b = pl.program_id(0)
    n_used = nused_ref[0]
    used = b < n_used
    slot = b % 2
    p = MOE_PHASE_STEPS
    n_chunks = 3 * p
    d_e = w_buf.shape[1]

    def chunk_copy(blk, c):
        e = exp_ref[blk]
        if c < 2 * p:
            src = (wg_hbm if c % 2 == 0 else wu_hbm).at[layer, e, pl.ds((c // 2) * d_e, d_e), :]
        else:
            src = wd_hbm.at[layer, e, :, pl.ds((c - 2 * p) * d_e, d_e)]
        ring = c % MOE_WEIGHT_SLOTS
        return pltpu.make_async_copy(src, w_buf.at[ring], w_sems.at[ring])

    def row_copy(tok_ref, r, slot_):
        return pltpu.make_async_copy(h_ref.at[pl.ds(tok_ref[0, r], 1)], x_buf.at[slot_, pl.ds(r, 1)],
                                     sems.at[slot_])

    def gather_start(tok_ref, n, slot_):
        def body(r, c):
            row_copy(tok_ref, r, slot_).start()
            return c

        lax.fori_loop(0, n, body, 0)

    def gather_wait(tok_ref, n, slot_):
        def body(r, c):
            row_copy(tok_ref, r, slot_).wait()
            return c

        lax.fori_loop(0, n, body, 0)

    @pl.when(b == 0)
    def _():
        for c in range(MOE_WEIGHT_LOOKAHEAD):
            chunk_copy(0, c).start()
        x_buf[...] = jnp.zeros(x_buf.shape, x_buf.dtype)
        gather_start(tok0_ref, cnt_ref[0], 0)

    @pl.when(used)
    def _():
        gather_wait(tok0_ref, cnt_ref[b], slot)

        @pl.when(b + 1 < n_used)
        def _():
            gather_start(tok_next_ref, cnt_ref[b + 1], 1 - slot)

        for c in range(n_chunks):
            chunk_copy(b, c).wait()
            ahead = c + MOE_WEIGHT_LOOKAHEAD
            if ahead < n_chunks:
                chunk_copy(b, ahead).start()
            else:
                @pl.when(b + 1 < n_used)
                def _(ahead=ahead):
                    chunk_copy(b + 1, ahead - n_chunks).start()

            w = _bf16(w_buf[c % MOE_WEIGHT_SLOTS])
            if c < 2 * p:
                k = c // 2
                x = _bf16(x_buf[slot, :, k * d_e:(k + 1) * d_e])
                part = jnp.dot(x, w, preferred_element_type=jnp.float32)
                acc = g_acc if c % 2 == 0 else u_acc
                if k == 0:
                    acc[...] = part
                else:
                    acc[...] += part
                if c == 2 * p - 1:
                    g = g_acc[...]
                    h_buf[...] = _bf16(g * jax.nn.sigmoid(g) * u_acc[...])
            else:
                n = c - 2 * p
                o_ref[:, n * d_e:(n + 1) * d_e] = jnp.dot(h_buf[...], w, preferred_element_type=jnp.float32)


def _experts(h, row_tok, w_gate, w_up, w_down, layer, blk_exp, blk_cnt, n_used):
    d = h.shape[1]
    d_e = w_gate.shape[3]
    n_blocks, _, block_rows = row_tok.shape
    n_rows = n_blocks * block_rows
    if d_e * MOE_PHASE_STEPS != d or (3 * MOE_PHASE_STEPS) % MOE_WEIGHT_SLOTS:
        raise ValueError("expert weight chunks must be square and fill the ring a whole number of times")

    def out_map(b, exp, cnt, nu):
        return (jnp.minimum(b, nu[0] - 1), 0)

    def tok_next_map(b, exp, cnt, nu):
        return (jnp.minimum(b + 1, nu[0] - 1), 0, 0)

    grid_spec = pltpu.PrefetchScalarGridSpec(
        num_scalar_prefetch=3,
        grid=(n_blocks,),
        in_specs=[pl.BlockSpec((None, 1, block_rows), lambda b, exp, cnt, nu: (0, 0, 0), memory_space=pltpu.SMEM),
                  pl.BlockSpec((None, 1, block_rows), tok_next_map, memory_space=pltpu.SMEM),
                  pl.BlockSpec(memory_space=pl.ANY),
                  pl.BlockSpec(memory_space=pl.ANY),
                  pl.BlockSpec(memory_space=pl.ANY),
                  pl.BlockSpec(memory_space=pl.ANY)],
        out_specs=pl.BlockSpec((block_rows, d), out_map),
        scratch_shapes=[pltpu.VMEM((2, block_rows, d), jnp.float32),
                        pltpu.SemaphoreType.DMA((2,)),
                        pltpu.VMEM((MOE_WEIGHT_SLOTS, d_e, d_e), jnp.float32),
                        pltpu.SemaphoreType.DMA((MOE_WEIGHT_SLOTS,)),
                        pltpu.VMEM((block_rows, d_e), jnp.float32),
                        pltpu.VMEM((block_rows, d_e), jnp.float32),
                        pltpu.VMEM((block_rows, d_e), MXU_DTYPE)],
    )
    blk_bytes = (block_rows * d * 4 * 2 + MOE_WEIGHT_SLOTS * d_e * d_e * 2 + d_e * d_e * 2 + block_rows * d_e * 5)
    return pl.pallas_call(
        functools.partial(_experts_kernel, layer=layer),
        grid_spec=grid_spec,
        out_shape=jax.ShapeDtypeStruct((n_rows, d), jnp.float32),
        compiler_params=_params(("arbitrary",), blk_bytes),
        name="moe_experts",
    )(blk_exp, blk_cnt, n_used, row_tok, row_tok, h, w_gate, w_up, w_down)


def _combine_kernel(dest_ref, y_ref, x_ref, wt_ref, g_ref, *refs, tr, total_rows, prompt_steps):
    out_refs, (ybuf, sems) = refs[:-2], refs[-2:]
    i = pl.program_id(0)
    slot = i % 2

    def copy(step, slot_, k, r):
        row = dest_ref[k * total_rows + step * tr + r]
        return pltpu.make_async_copy(y_ref.at[pl.ds(row, 1)], ybuf.at[slot_, k, pl.ds(r, 1)], sems.at[slot_])

    def gather_start(step, slot_):
        for k in range(TOP_K):
            def body(r, c, k=k):
                copy(step, slot_, k, r).start()
                return c

            lax.fori_loop(0, tr, body, 0)

    @pl.when(i == 0)
    def _():
        gather_start(0, 0)

    @pl.when(i + 1 < pl.num_programs(0))
    def _():
        gather_start(i + 1, 1 - slot)

    for k in range(TOP_K):
        def wait(r, c, k=k):
            copy(i, slot, k, r).wait()
            return c

        lax.fori_loop(0, tr, wait, 0)

    def emit(o_ref):
        for r in range(0, tr, V7X_SUBLANES):
            gi = r // V7X_SUBLANES
            rs = slice(r, r + V7X_SUBLANES)
            y = wt_ref[rs, 0:1] * ybuf[slot, 0, rs, :] + wt_ref[rs, 1:2] * ybuf[slot, 1, rs, :]
            o_ref[rs, :] = x_ref[rs, :] + g_ref[gi:gi + 1, :] * y

    if prompt_steps is None:
        emit(out_refs[0])
    else:
        pl.when(i < prompt_steps)(lambda: emit(out_refs[0]))
        pl.when(i >= prompt_steps)(lambda: emit(out_refs[1]))


def _combine(yb, dest, wts, x, modg, *, gate_col, split_rows=None):
    rows, d = x.shape
    if split_rows is None:
        tr = _pick_tile(rows, 128, V7X_SUBLANES)
        p_steps = None
        out_specs = pl.BlockSpec((tr, d), lambda i, dst: (i, 0))
        out_shape = jax.ShapeDtypeStruct((rows, d), jnp.float32)
    else:
        tr = _pick_tile(math.gcd(split_rows, rows - split_rows), 128, V7X_SUBLANES)
        p_steps = split_rows // tr
        out_specs = [pl.BlockSpec((tr, d), lambda i, dst: (jnp.minimum(i, p_steps - 1), 0)),
                     pl.BlockSpec((tr, d), lambda i, dst: (jnp.maximum(i - p_steps, 0), 0))]
        out_shape = [jax.ShapeDtypeStruct((split_rows, d), jnp.float32),
                     jax.ShapeDtypeStruct((rows - split_rows, d), jnp.float32)]
    kern = functools.partial(_combine_kernel, tr=tr, total_rows=rows, prompt_steps=p_steps)
    grid_spec = pltpu.PrefetchScalarGridSpec(
        num_scalar_prefetch=1,
        grid=(rows // tr,),
        in_specs=[pl.BlockSpec(memory_space=pl.ANY),
                  pl.BlockSpec((tr, d), lambda i, dst: (i, 0)),
                  pl.BlockSpec((tr, TOP_K), lambda i, dst: (i, 0)),
                  pl.BlockSpec((tr // V7X_SUBLANES, d), lambda i, dst: (i, gate_col))],
        out_specs=out_specs,
        scratch_shapes=[pltpu.VMEM((2, TOP_K, tr, d), yb.dtype),
                        pltpu.SemaphoreType.DMA((2,))],
    )
    return pl.pallas_call(
        kern,
        grid_spec=grid_spec,
        out_shape=out_shape,
        compiler_params=_params(("arbitrary",), tr * d * 20),
        name="moe_combine",
    )(dest.reshape(-1), yb, x, wts.T, modg)


def kernel(x_prompt, x_sample, cache_k, cache_v, state_conv, c_prompt, c_sample, router_w, router_b, norm1_g, norm2_g, w_ada, b_ada, w_in, q_norm_g, k_norm_g, attn_sink, w_ao, dw_w, dw_b, cn_g, cn_b, w_co, w_out, moe_w_gate, moe_w_up, moe_w_down):
    batch, seq, d = x_prompt.shape
    dec_batch, t_new, _ = x_sample.shape
    depth = w_in.shape[0]
    window = cache_k.shape[2]
    n_kv = cache_k.shape[3]
    d_kv = n_kv * cache_k.shape[4]
    d_q = w_ao.shape[1]
    d_conv = w_co.shape[1]
    n_experts = router_w.shape[1]
    if q_norm_g.shape[1] != HEAD_DIM or cache_k.shape[4] != HEAD_DIM or dw_w.shape[1] != CONV_WIDTH:
        raise ValueError("unsupported head / conv geometry")
    if d_q != Q_PER_KV * d_kv or n_experts % N_GROUPS:
        raise ValueError("unsupported head / expert grouping")
    if seq % window or t_new % V7X_SUBLANES or t_new > window:
        raise ValueError("unsupported sequence geometry")
    prompt_rows = batch * seq
    sample_rows = dec_batch * t_new
    rows = prompt_rows + sample_rows
    ga_off = d_q + 2 * d_kv + 2 * d_conv

    n_seq = batch + dec_batch
    pad = -n_seq % V7X_SUBLANES
    c_all = jnp.concatenate([c_prompt, c_sample, jnp.zeros((pad, d), jnp.float32)], axis=0)
    mod = _adaln(c_all, w_ada, b_ada)
    modg = jnp.concatenate([jnp.repeat(mod[:, :batch], seq // V7X_SUBLANES, axis=1),
                            jnp.repeat(mod[:, batch:n_seq], t_new // V7X_SUBLANES, axis=1)], axis=1)
    SH1, SC1, G1, SH2, SC2, G2 = range(6)

    kp, vp, up, ks, vs, us = [], [], [], [], [], []
    for l in range(depth):
        mg = modg[l]
        if l == 0:
            h, x = _norm_mod_stack(x_prompt.reshape(prompt_rows, d), x_sample.reshape(sample_rows, d),
                                   norm1_g[l], mg, SH1, SC1)
        else:
            h = _norm_mod(x, norm1_g[l], mg, SH1, SC1)
        proj = _in_proj(h, w_in, l)

        a, k_last = _attn_prompt(proj, attn_sink[l], q_norm_g[l], k_norm_g[l], batch=batch, seq=seq,
                                 total_rows=rows, d_q=d_q, d_kv=d_kv, window=window)
        a, k_new = _attn_sample(a, proj, cache_k[l], cache_v[l], attn_sink[l], q_norm_g[l], k_norm_g[l],
                                prompt_rows=prompt_rows, dec_batch=dec_batch, t_new=t_new,
                                d_q=d_q, d_kv=d_kv, window=window)
        cv, u_last = _conv_prompt(proj, dw_w[l], dw_b[l], cn_g[l], cn_b[l], batch=batch, seq=seq,
                                  total_rows=rows, d_q=d_q, d_kv=d_kv, d_conv=d_conv)
        cv, u_new = _conv_sample(cv, proj, state_conv[l], dw_w[l], dw_b[l], cn_g[l], cn_b[l],
                                 prompt_rows=prompt_rows, dec_batch=dec_batch, t_new=t_new,
                                 d_q=d_q, d_kv=d_kv, d_conv=d_conv)

        z = _merge(a, cv, proj, w_ao, w_co, l, ga_off=ga_off)
        x = _out_proj(z, w_out, x, mg, l, gate_col=G1)

        h2, eid, wts = _norm_router(x, norm2_g[l], mg, router_w, router_b, shift_col=SH2, scale_col=SC2)
        row_tok, dest, blk_exp, blk_cnt, n_used = _dispatch_plan(eid, n_experts, MOE_BLOCK_ROWS)
        yb = _experts(h2, row_tok, moe_w_gate, moe_w_up, moe_w_down, l, blk_exp, blk_cnt, n_used)
        if l + 1 < depth:
            x = _combine(yb, dest, wts, x, mg, gate_col=G2)
        else:
            y_prompt, y_sample = _combine(yb, dest, wts, x, mg, gate_col=G2, split_rows=prompt_rows)

        v_cols = slice(d_q + d_kv, d_q + 2 * d_kv)
        v_prompt = proj[:prompt_rows, v_cols].reshape(batch, seq, d_kv)[:, seq - window:]
        v_samp = proj[prompt_rows:, v_cols].reshape(dec_batch, t_new, d_kv)
        kp.append(k_last.reshape(batch, window, n_kv, HEAD_DIM))
        vp.append(v_prompt.reshape(batch, window, n_kv, HEAD_DIM))
        up.append(u_last)
        ks.append(jnp.concatenate([cache_k[l][:, t_new:],
                                   k_new.reshape(dec_batch, t_new, n_kv, HEAD_DIM)], axis=1))
        vs.append(jnp.concatenate([cache_v[l][:, t_new:],
                                   v_samp.reshape(dec_batch, t_new, n_kv, HEAD_DIM)], axis=1))
        us.append(u_new)

    y_prompt = y_prompt.reshape(batch, seq, d)
    y_sample = y_sample.reshape(dec_batch, t_new, d)
    return (y_prompt, y_sample, jnp.stack(kp), jnp.stack(vp), jnp.stack(up),
            jnp.stack(ks), jnp.stack(vs), jnp.stack(us))
```

```python
import functools
import math

import jax
import jax.numpy as jnp
from jax import lax
from jax.experimental import pallas as pl
from jax.experimental.pallas import tpu as pltpu

EPS = 1e-6
HEAD_DIM = 128
Q_PER_KV = 4
CONV_WIDTH = 31
N_GROUPS = 8
TOP_K = 2

V7X_SUBLANES = 8
V7X_LANES = 128
V7X_VMEM_BYTES = 64 * 1024 * 1024
_MIB = 1024 * 1024
_BF16_ROWS = 2 * V7X_SUBLANES
MXU_DTYPE = jnp.bfloat16

CONV_HALO = 32
CONV_LEAD = CONV_HALO - (CONV_WIDTH - 1)
CONV_ROW_CHUNK = 64
CONV_CH_CHUNK = 256
MOE_BLOCK_ROWS = 384
MOE_PHASE_STEPS = 4
MOE_WEIGHT_SLOTS = 4
MOE_WEIGHT_LOOKAHEAD = MOE_WEIGHT_SLOTS - 1
DMA_ISSUE_SHIFT = 3
DMA_ISSUE_UNROLL = 1 << DMA_ISSUE_SHIFT


def _bf16(x):
    return x.astype(MXU_DTYPE)


def _pick_tile(n, target, align):
    best = None
    for t in range(align, min(n, target) + 1, align):
        if n % t == 0:
            best = t
    if best is None:
        raise ValueError(f"no tile for n={n} target={target} align={align}")
    return best


def _exact_div(a, b):
    q, r = divmod(a, b)
    if r:
        raise ValueError(f"{a} is not a multiple of {b}")
    return q


def _params(semantics, block_bytes):
    need = 2 * block_bytes + 8 * _MIB
    limit = int(min(max(need, 16 * _MIB), V7X_VMEM_BYTES - 8 * _MIB))
    return pltpu.CompilerParams(dimension_semantics=semantics, vmem_limit_bytes=limit)


def _adaln_kernel(c_ref, w_ref, b_ref, o_ref, *, batch, sample_row0, dec_batch, prompt_groups, sample_groups):
    c = c_ref[...]
    a = _bf16(c * jax.nn.sigmoid(c))
    mod = jnp.dot(a, _bf16(w_ref[...]), preferred_element_type=jnp.float32) + b_ref[...]
    tn = mod.shape[1]
    for i in range(batch):
        o_ref[i * prompt_groups:(i + 1) * prompt_groups, :] = jnp.broadcast_to(mod[i:i + 1, :], (prompt_groups, tn))
    base = batch * prompt_groups
    if sample_groups == 1:
        o_ref[base:base + dec_batch, :] = mod[sample_row0:sample_row0 + dec_batch, :]
    else:
        for j in range(dec_batch):
            o_ref[base + j * sample_groups:base + (j + 1) * sample_groups, :] = jnp.broadcast_to(
                mod[sample_row0 + j:sample_row0 + j + 1, :], (sample_groups, tn))


def _adaln(c_prompt, c_sample, w_ada, b_ada, layer, *, prompt_groups, sample_groups):
    depth, d, n = w_ada.shape
    batch, dec_batch = c_prompt.shape[0], c_sample.shape[0]
    sample_row0 = -(-batch // V7X_SUBLANES) * V7X_SUBLANES
    c_all = jnp.concatenate([c_prompt, jnp.zeros((sample_row0 - batch, d), jnp.float32), c_sample,
                             jnp.zeros((-dec_batch % V7X_SUBLANES, d), jnp.float32)], axis=0)
    rows = c_all.shape[0]
    groups = batch * prompt_groups + dec_batch * sample_groups
    tn = _pick_tile(n, 1024, V7X_LANES)
    kern = functools.partial(_adaln_kernel, batch=batch, sample_row0=sample_row0, dec_batch=dec_batch,
                             prompt_groups=prompt_groups, sample_groups=sample_groups)
    return pl.pallas_call(
        kern,
        grid=(n // tn,),
        in_specs=[pl.BlockSpec((rows, d), lambda j: (0, 0)),
                  pl.BlockSpec((None, d, tn), lambda j: (layer, 0, j)),
                  pl.BlockSpec((None, 1, tn), lambda j: (layer, 0, j))],
        out_specs=pl.BlockSpec((groups, tn), lambda j: (0, j)),
        out_shape=jax.ShapeDtypeStruct((groups, n), jnp.float32),
        compiler_params=_params(("arbitrary",), d * tn * 6 + rows * d * 4 + groups * tn * 4),
        name="adaln",
    )(c_all, w_ada, b_ada.reshape(depth, 1, n))


def _modulated_norm_rows(x_ref, g_ref, sc_ref, sh_ref, row0, n_rows):
    g = g_ref[...]
    parts = []
    for r in range(row0, row0 + n_rows, V7X_SUBLANES):
        x = x_ref[r:r + V7X_SUBLANES, :]
        gi = r // V7X_SUBLANES
        y = x * lax.rsqrt(jnp.mean(x * x, axis=-1, keepdims=True) + EPS) * g
        parts.append(y * (1.0 + sc_ref[gi:gi + 1, :]) + sh_ref[gi:gi + 1, :])
    return parts[0] if len(parts) == 1 else jnp.concatenate(parts, axis=0)


def _norm_mod_kernel(x_ref, g_ref, sc_ref, sh_ref, o_ref):
    for r in range(0, o_ref.shape[0], _BF16_ROWS):
        o_ref[r:r + _BF16_ROWS, :] = _bf16(_modulated_norm_rows(x_ref, g_ref, sc_ref, sh_ref, r, _BF16_ROWS))


def _norm_mod(x, g, modg, shift_col, scale_col):
    rows, d = x.shape
    tr = _pick_tile(rows, 256, _BF16_ROWS)
    tg = tr // V7X_SUBLANES
    return pl.pallas_call(
        _norm_mod_kernel,
        grid=(rows // tr,),
        in_specs=[pl.BlockSpec((tr, d), lambda i: (i, 0)),
                  pl.BlockSpec((1, d), lambda i: (0, 0)),
                  pl.BlockSpec((tg, d), lambda i: (i, scale_col)),
                  pl.BlockSpec((tg, d), lambda i: (i, shift_col))],
        out_specs=pl.BlockSpec((tr, d), lambda i: (i, 0)),
        out_shape=jax.ShapeDtypeStruct((rows, d), MXU_DTYPE),
        compiler_params=_params(("arbitrary",), tr * d * 8),
        name="norm_mod",
    )(x, g.reshape(1, d), modg, modg)


def _norm_mod_stack_kernel(xp_ref, xs_ref, g_ref, sc_ref, sh_ref, o_ref, x_ref, *, prompt_steps):
    def emit(src_ref):
        x_ref[...] = src_ref[...]
        _norm_mod_kernel(src_ref, g_ref, sc_ref, sh_ref, o_ref)

    is_prompt = pl.program_id(0) < prompt_steps
    pl.when(is_prompt)(lambda: emit(xp_ref))
    pl.when(jnp.logical_not(is_prompt))(lambda: emit(xs_ref))


def _norm_mod_stack(x_prompt, x_sample, g, modg, shift_col, scale_col):
    p_rows, d = x_prompt.shape
    s_rows = x_sample.shape[0]
    rows = p_rows + s_rows
    tr = _pick_tile(math.gcd(p_rows, s_rows), 256, _BF16_ROWS)
    tg = tr // V7X_SUBLANES
    p_steps = p_rows // tr
    kern = functools.partial(_norm_mod_stack_kernel, prompt_steps=p_steps)
    return pl.pallas_call(
        kern,
        grid=(rows // tr,),
        in_specs=[pl.BlockSpec((tr, d), lambda i: (jnp.minimum(i, p_steps - 1), 0)),
                  pl.BlockSpec((tr, d), lambda i: (jnp.maximum(i - p_steps, 0), 0)),
                  pl.BlockSpec((1, d), lambda i: (0, 0)),
                  pl.BlockSpec((tg, d), lambda i: (i, scale_col)),
                  pl.BlockSpec((tg, d), lambda i: (i, shift_col))],
        out_specs=[pl.BlockSpec((tr, d), lambda i: (i, 0)),
                   pl.BlockSpec((tr, d), lambda i: (i, 0))],
        out_shape=[jax.ShapeDtypeStruct((rows, d), MXU_DTYPE),
                   jax.ShapeDtypeStruct((rows, d), jnp.float32)],
        compiler_params=_params(("arbitrary",), tr * d * 16),
        name="norm_mod_stack",
    )(x_prompt, x_sample, g.reshape(1, d), modg, modg)


def _matmul_kernel(x_ref, w_ref, o_ref):
    o_ref[...] = jnp.dot(x_ref[...], _bf16(w_ref[...]), preferred_element_type=jnp.float32)


def _in_proj(h, w_in, layer):
    rows, k = h.shape
    n = w_in.shape[2]
    tm = _pick_tile(rows, 1408, _BF16_ROWS)
    tn = _pick_tile(n, 512, V7X_LANES)
    return pl.pallas_call(
        _matmul_kernel,
        grid=(n // tn, rows // tm),
        in_specs=[pl.BlockSpec((tm, k), lambda j, i: (i, 0)),
                  pl.BlockSpec((None, k, tn), lambda j, i: (layer, 0, j))],
        out_specs=pl.BlockSpec((tm, tn), lambda j, i: (i, j)),
        out_shape=jax.ShapeDtypeStruct((rows, n), jnp.float32),
        compiler_params=_params(("arbitrary", "arbitrary"), tm * k * 2 + k * tn * 6 + tm * tn * 4),
        name="in_proj",
    )(h, w_in)


def _head_rms(x, g):
    return x * lax.rsqrt(jnp.mean(x * x, axis=-1, keepdims=True) + EPS) * g


def _window_mask(n_q, window, older_present):
    n_rows = Q_PER_KV * n_q
    row = lax.broadcasted_iota(jnp.int32, (n_rows, 2 * window), 0) % n_q
    col = lax.broadcasted_iota(jnp.int32, (n_rows, 2 * window), 1)
    return (col >= row) & (col <= row + window) & ((col >= window) | older_present)


def _kv_group_attention(q_heads, k_all, v_all, sink_vals, valid, n_q):
    q_all = jnp.concatenate(q_heads, axis=0)
    sink_col = jnp.concatenate([jnp.full((n_q, 1), sv, jnp.float32) for sv in sink_vals], axis=0)
    s = lax.dot_general(_bf16(q_all), _bf16(k_all), (((1,), (1,)), ((), ())),
                        preferred_element_type=jnp.float32) * (1.0 / math.sqrt(HEAD_DIM))
    s = jnp.where(valid, s, -jnp.inf)
    m = jnp.maximum(jnp.max(s, axis=-1, keepdims=True), sink_col)
    p = jnp.exp(s - m)
    denom = jnp.sum(p, axis=-1, keepdims=True) + jnp.exp(sink_col - m)
    return jnp.dot(_bf16(p / denom), _bf16(v_all), preferred_element_type=jnp.float32)


def _attn_prompt_kernel(sink_ref, q_ref, kp_ref, kc_ref, vp_ref, vc_ref, qg_ref, kg_ref, o_ref, kn_ref,
                        vn_ref, *, n_kv, window):
    qg = qg_ref[...]
    kg = kg_ref[...]
    valid = _window_mask(window, window, pl.program_id(1) > 0)
    vn_ref[...] = vc_ref[...]
    for h in range(n_kv):
        ksl = slice(h * HEAD_DIM, (h + 1) * HEAD_DIM)
        k_cur = _head_rms(kc_ref[:, ksl], kg)
        kn_ref[:, ksl] = k_cur
        k_all = jnp.concatenate([_head_rms(kp_ref[:, ksl], kg), k_cur], axis=0)
        v_all = jnp.concatenate([vp_ref[:, ksl], vc_ref[:, ksl]], axis=0)
        heads = [h * Q_PER_KV + g for g in range(Q_PER_KV)]
        q_heads = [_head_rms(q_ref[:, hq * HEAD_DIM:(hq + 1) * HEAD_DIM], qg) for hq in heads]
        o = _kv_group_attention(q_heads, k_all, v_all, [sink_ref[hq] for hq in heads], valid, window)
        for g, hq in enumerate(heads):
            o_ref[:, hq * HEAD_DIM:(hq + 1) * HEAD_DIM] = _bf16(o[g * window:(g + 1) * window])


def _attn_prompt(proj, sink, q_g, k_g, *, batch, seq, total_rows, d_q, d_kv, window):
    n_kv = d_kv // HEAD_DIM
    nb = _exact_div(seq, window)
    k_col = _exact_div(d_q, d_kv)
    v_col = k_col + 1
    kern = functools.partial(_attn_prompt_kernel, n_kv=n_kv, window=window)

    def cur(b, i, s):
        return b * nb + i

    def prev(b, i, s):
        return b * nb + jnp.maximum(i - 1, 0)

    grid_spec = pltpu.PrefetchScalarGridSpec(
        num_scalar_prefetch=1,
        grid=(batch, nb),
        in_specs=[pl.BlockSpec((window, d_q), lambda b, i, s: (cur(b, i, s), 0)),
                  pl.BlockSpec((window, d_kv), lambda b, i, s: (prev(b, i, s), k_col)),
                  pl.BlockSpec((window, d_kv), lambda b, i, s: (cur(b, i, s), k_col)),
                  pl.BlockSpec((window, d_kv), lambda b, i, s: (prev(b, i, s), v_col)),
                  pl.BlockSpec((window, d_kv), lambda b, i, s: (cur(b, i, s), v_col)),
                  pl.BlockSpec((1, HEAD_DIM), lambda b, i, s: (0, 0)),
                  pl.BlockSpec((1, HEAD_DIM), lambda b, i, s: (0, 0))],
        out_specs=[pl.BlockSpec((window, d_q), lambda b, i, s: (cur(b, i, s), 0)),
                   pl.BlockSpec((None, window, d_kv), lambda b, i, s: (b, 0, 0)),
                   pl.BlockSpec((None, window, d_kv), lambda b, i, s: (b, 0, 0))],
    )
    return pl.pallas_call(
        kern,
        grid_spec=grid_spec,
        out_shape=[jax.ShapeDtypeStruct((total_rows, d_q), MXU_DTYPE),
                   jax.ShapeDtypeStruct((batch, window, d_kv), jnp.float32),
                   jax.ShapeDtypeStruct((batch, window, d_kv), jnp.float32)],
        compiler_params=_params(("arbitrary", "arbitrary"), window * (d_q * 6 + d_kv * 28)),
        name="attn_prompt",
    )(sink, proj, proj, proj, proj, proj, q_g.reshape(1, HEAD_DIM), k_g.reshape(1, HEAD_DIM))


def _attn_sample_kernel(sink_ref, a_in_ref, q_ref, kn_ref, vn_ref, kc_ref, vc_ref, qg_ref, kg_ref,
                        o_ref, kout_ref, vout_ref, *, n_kv, window, t_new, seqs):
    del a_in_ref
    qg = qg_ref[...]
    kg = kg_ref[...]
    valid = _window_mask(t_new, window, True)
    fill = jnp.zeros((window - t_new, HEAD_DIM), jnp.float32)
    keep = window - t_new
    for sq in range(seqs):
        rows = slice(sq * t_new, (sq + 1) * t_new)
        kout_ref[sq, 0:keep, :] = kc_ref[sq, t_new:window, :]
        vout_ref[sq, 0:keep, :] = vc_ref[sq, t_new:window, :]
        vout_ref[sq, keep:window, :] = vn_ref[rows, :]
        for h in range(n_kv):
            ksl = slice(h * HEAD_DIM, (h + 1) * HEAD_DIM)
            k_new = _head_rms(kn_ref[rows, ksl], kg)
            kout_ref[sq, keep:window, ksl] = k_new
            k_all = jnp.concatenate([kc_ref[sq, :, ksl], k_new, fill], axis=0)
            v_all = jnp.concatenate([vc_ref[sq, :, ksl], vn_ref[rows, ksl], fill], axis=0)
            heads = [h * Q_PER_KV + g for g in range(Q_PER_KV)]
            q_heads = [_head_rms(q_ref[rows, hq * HEAD_DIM:(hq + 1) * HEAD_DIM], qg) for hq in heads]
            o = _kv_group_attention(q_heads, k_all, v_all, [sink_ref[hq] for hq in heads], valid, t_new)
            for g, hq in enumerate(heads):
                o_ref[rows, hq * HEAD_DIM:(hq + 1) * HEAD_DIM] = _bf16(o[g * t_new:(g + 1) * t_new])


def _attn_sample(a, proj, cache_k, cache_v, layer, sink, q_g, k_g, *, prompt_rows, dec_batch, t_new,
                 d_q, d_kv, window):
    n_kv = d_kv // HEAD_DIM
    depth = cache_k.shape[0]
    seqs = _exact_div(_BF16_ROWS, t_new) if _BF16_ROWS % t_new == 0 and dec_batch % (_BF16_ROWS // t_new) == 0 else 1
    tr = seqs * t_new
    row0 = _exact_div(prompt_rows, tr)
    k_col = _exact_div(d_q, d_kv)
    kern = functools.partial(_attn_sample_kernel, n_kv=n_kv, window=window, t_new=t_new, seqs=seqs)
    kc = cache_k.reshape(depth, dec_batch, window, d_kv)
    vc = cache_v.reshape(depth, dec_batch, window, d_kv)
    buf_spec = pl.BlockSpec((seqs, window, d_kv), lambda i, s: (i, 0, 0))
    grid_spec = pltpu.PrefetchScalarGridSpec(
        num_scalar_prefetch=1,
        grid=(dec_batch // seqs,),
        in_specs=[pl.BlockSpec(memory_space=pl.ANY),
                  pl.BlockSpec((tr, d_q), lambda i, s: (row0 + i, 0)),
                  pl.BlockSpec((tr, d_kv), lambda i, s: (row0 + i, k_col)),
                  pl.BlockSpec((tr, d_kv), lambda i, s: (row0 + i, k_col + 1)),
                  pl.BlockSpec((None, seqs, window, d_kv), lambda i, s: (layer, i, 0, 0)),
                  pl.BlockSpec((None, seqs, window, d_kv), lambda i, s: (layer, i, 0, 0)),
                  pl.BlockSpec((1, HEAD_DIM), lambda i, s: (0, 0)),
                  pl.BlockSpec((1, HEAD_DIM), lambda i, s: (0, 0))],
        out_specs=[pl.BlockSpec((tr, d_q), lambda i, s: (row0 + i, 0)), buf_spec, buf_spec],
    )
    buf_shape = jax.ShapeDtypeStruct((dec_batch, window, d_kv), jnp.float32)
    return pl.pallas_call(
        kern,
        grid_spec=grid_spec,
        out_shape=[jax.ShapeDtypeStruct(a.shape, a.dtype), buf_shape, buf_shape],
        input_output_aliases={1: 0},
        compiler_params=_params(("arbitrary",), tr * (d_q * 6 + d_kv * 8) + seqs * window * d_kv * 16),
        name="attn_sample",
    )(sink, a, proj, proj, proj, kc, vc, q_g.reshape(1, HEAD_DIM), k_g.reshape(1, HEAD_DIM))


def _depthwise_conv(uext_ref, base, n_rows, w_ref, b_ref, conv_ref, conv_base, part_ref):
    d_conv = conv_ref.shape[-1]
    ch_chunk = part_ref.shape[-1]
    for c0 in range(0, d_conv, ch_chunk):
        cs = slice(c0, c0 + ch_chunk)
        for s in range(V7X_SUBLANES):
            span = n_rows + -(-(CONV_LEAD + s) // V7X_SUBLANES) * V7X_SUBLANES
            for t0 in range(0, span, CONV_ROW_CHUNK):
                rc = min(CONV_ROW_CHUNK, span - t0)
                acc = None
                for j in range(s, CONV_WIDTH, V7X_SUBLANES):
                    start = base + t0 + j - s
                    term = uext_ref[start:start + rc, cs] * w_ref[j:j + 1, cs]
                    acc = term if acc is None else acc + term
                part_ref[s, t0:t0 + rc, :] = acc
        for r0 in range(0, n_rows, CONV_ROW_CHUNK):
            rc = min(CONV_ROW_CHUNK, n_rows - r0)
            acc = jnp.broadcast_to(b_ref[:, cs], (rc, ch_chunk))
            for s in range(V7X_SUBLANES):
                off = r0 + CONV_LEAD + s
                acc = acc + part_ref[s, off:off + rc, :]
            conv_ref[conv_base + r0:conv_base + r0 + rc, cs] = acc


def _conv_ch_chunk(d_conv):
    return _pick_tile(d_conv, CONV_CH_CHUNK, V7X_LANES)


def _layer_norm_swish(x, g, b):
    mu = jnp.mean(x, axis=-1, keepdims=True)
    xc = x - mu
    var = jnp.mean(xc * xc, axis=-1, keepdims=True)
    y = xc * lax.rsqrt(var + EPS) * g + b
    return y * jax.nn.sigmoid(y)


def _conv_prompt_kernel(uv_ref, ug_ref, w_ref, b_ref, g_ref, bb_ref, o_ref, st_ref, uext_ref, conv_ref,
                        part_ref, *, tt):
    @pl.when(pl.program_id(1) == 0)
    def _():
        uext_ref[0:CONV_HALO, :] = jnp.zeros((CONV_HALO, uext_ref.shape[1]), jnp.float32)

    uext_ref[CONV_HALO:CONV_HALO + tt, :] = uv_ref[...] * jax.nn.sigmoid(ug_ref[...])
    _depthwise_conv(uext_ref, 0, tt, w_ref, b_ref, conv_ref, 0, part_ref)
    o_ref[...] = _bf16(_layer_norm_swish(conv_ref[...], g_ref[...], bb_ref[...]))
    st_ref[...] = uext_ref[tt + CONV_LEAD:tt + CONV_HALO, :]
    uext_ref[0:CONV_HALO, :] = uext_ref[tt:tt + CONV_HALO, :]


def _conv_prompt(proj, dw_w, dw_b, cn_g, cn_b, *, batch, seq, total_rows, d_q, d_kv, d_conv):
    tt = _pick_tile(seq, 256, 32)
    nt = seq // tt
    uv_col = _exact_div(d_q + 2 * d_kv, d_conv)
    kern = functools.partial(_conv_prompt_kernel, tt=tt)
    return pl.pallas_call(
        kern,
        grid=(batch, nt),
        in_specs=[pl.BlockSpec((tt, d_conv), lambda b, t: (b * nt + t, uv_col)),
                  pl.BlockSpec((tt, d_conv), lambda b, t: (b * nt + t, uv_col + 1)),
                  pl.BlockSpec((CONV_WIDTH, d_conv), lambda b, t: (0, 0)),
                  pl.BlockSpec((1, d_conv), lambda b, t: (0, 0)),
                  pl.BlockSpec((1, d_conv), lambda b, t: (0, 0)),
                  pl.BlockSpec((1, d_conv), lambda b, t: (0, 0))],
        out_specs=[pl.BlockSpec((tt, d_conv), lambda b, t: (b * nt + t, 0)),
                   pl.BlockSpec((None, CONV_WIDTH - 1, d_conv), lambda b, t: (b, 0, 0))],
        out_shape=[jax.ShapeDtypeStruct((total_rows, d_conv), MXU_DTYPE),
                   jax.ShapeDtypeStruct((batch, CONV_WIDTH - 1, d_conv), jnp.float32)],
        scratch_shapes=[pltpu.VMEM((tt + CONV_HALO, d_conv), jnp.float32),
                        pltpu.VMEM((tt, d_conv), jnp.float32),
                        pltpu.VMEM((V7X_SUBLANES, tt + 2 * V7X_SUBLANES, _conv_ch_chunk(d_conv)), jnp.float32)],
        compiler_params=_params(("arbitrary", "arbitrary"), tt * d_conv * 18),
        name="conv_prompt",
    )(proj, proj, dw_w, dw_b.reshape(1, d_conv), cn_g.reshape(1, d_conv), cn_b.reshape(1, d_conv))


def _conv_sample_kernel(cv_in_ref, uv_ref, ug_ref, st_ref, w_ref, b_ref, g_ref, bb_ref, o_ref, nst_ref,
                        uext_ref, conv_ref, part_ref, *, t_new, seqs):
    del cv_in_ref
    ext = CONV_HALO + t_new
    for sq in range(seqs):
        base = sq * ext
        rows = slice(sq * t_new, (sq + 1) * t_new)
        uext_ref[base:base + CONV_LEAD, :] = jnp.zeros((CONV_LEAD, uext_ref.shape[1]), jnp.float32)
        uext_ref[base + CONV_LEAD:base + CONV_HALO, :] = st_ref[sq]
        uext_ref[base + CONV_HALO:base + ext, :] = uv_ref[rows, :] * jax.nn.sigmoid(ug_ref[rows, :])
        nst_ref[sq] = uext_ref[base + ext - (CONV_WIDTH - 1):base + ext, :]
        _depthwise_conv(uext_ref, base, t_new, w_ref, b_ref, conv_ref, sq * t_new, part_ref)
    o_ref[...] = _bf16(_layer_norm_swish(conv_ref[...], g_ref[...], bb_ref[...]))


def _conv_sample(cv, proj, state_conv, layer, dw_w, dw_b, cn_g, cn_b, *, prompt_rows, dec_batch, t_new,
                 d_q, d_kv, d_conv):
    seqs = _exact_div(_BF16_ROWS, t_new) if _BF16_ROWS % t_new == 0 and dec_batch % (_BF16_ROWS // t_new) == 0 else 1
    tr = seqs * t_new
    row0 = _exact_div(prompt_rows, tr)
    uv_col = _exact_div(d_q + 2 * d_kv, d_conv)
    kern = functools.partial(_conv_sample_kernel, t_new=t_new, seqs=seqs)
    return pl.pallas_call(
        kern,
        grid=(dec_batch // seqs,),
        in_specs=[pl.BlockSpec(memory_space=pl.ANY),
                  pl.BlockSpec((tr, d_conv), lambda i: (row0 + i, uv_col)),
                  pl.BlockSpec((tr, d_conv), lambda i: (row0 + i, uv_col + 1)),
                  pl.BlockSpec((None, seqs, CONV_WIDTH - 1, d_conv), lambda i: (layer, i, 0, 0)),
                  pl.BlockSpec((CONV_WIDTH, d_conv), lambda i: (0, 0)),
                  pl.BlockSpec((1, d_conv), lambda i: (0, 0)),
                  pl.BlockSpec((1, d_conv), lambda i: (0, 0)),
                  pl.BlockSpec((1, d_conv), lambda i: (0, 0))],
        out_specs=[pl.BlockSpec((tr, d_conv), lambda i: (row0 + i, 0)),
                   pl.BlockSpec((seqs, CONV_WIDTH - 1, d_conv), lambda i: (i, 0, 0))],
        out_shape=[jax.ShapeDtypeStruct(cv.shape, cv.dtype),
                   jax.ShapeDtypeStruct((dec_batch, CONV_WIDTH - 1, d_conv), jnp.float32)],
        input_output_aliases={0: 0},
        scratch_shapes=[pltpu.VMEM((seqs * (CONV_HALO + t_new), d_conv), jnp.float32),
                        pltpu.VMEM((tr, d_conv), jnp.float32),
                        pltpu.VMEM((V7X_SUBLANES, t_new + 2 * V7X_SUBLANES, _conv_ch_chunk(d_conv)), jnp.float32)],
        compiler_params=_params(("arbitrary",), (tr * 12 + seqs * 64 * 4 + 40 * 4) * d_conv),
        name="conv_sample",
    )(cv, proj, proj, state_conv, dw_w, dw_b.reshape(1, d_conv), cn_g.reshape(1, d_conv), cn_b.reshape(1, d_conv))


def _merge_kernel(a_ref, cv_ref, wao_ref, wco_ref, ga_ref, gb_ref, o_ref):
    pa = jnp.dot(a_ref[...], _bf16(wao_ref[...]), preferred_element_type=jnp.float32)
    pc = jnp.dot(cv_ref[...], _bf16(wco_ref[...]), preferred_element_type=jnp.float32)
    o_ref[...] = _bf16(jax.nn.sigmoid(ga_ref[...]) * pa + jax.nn.sigmoid(gb_ref[...]) * pc)


def _merge(a, cv, proj, w_ao, w_co, layer, *, ga_off):
    rows, d_q = a.shape
    d_conv = cv.shape[1]
    d = w_ao.shape[2]
    tm = _pick_tile(rows, 384, _BF16_ROWS)
    tn = _pick_tile(d, 512, V7X_LANES)
    ga_col = _exact_div(ga_off, tn)
    gb_col = ga_col + d // tn
    blk = tm * (d_q + d_conv) * 2 + (d_q + d_conv) * tn * 6 + tm * tn * 10
    return pl.pallas_call(
        _merge_kernel,
        grid=(d // tn, rows // tm),
        in_specs=[pl.BlockSpec((tm, d_q), lambda j, i: (i, 0)),
                  pl.BlockSpec((tm, d_conv), lambda j, i: (i, 0)),
                  pl.BlockSpec((None, d_q, tn), lambda j, i: (layer, 0, j)),
                  pl.BlockSpec((None, d_conv, tn), lambda j, i: (layer, 0, j)),
                  pl.BlockSpec((tm, tn), lambda j, i: (i, ga_col + j)),
                  pl.BlockSpec((tm, tn), lambda j, i: (i, gb_col + j))],
        out_specs=pl.BlockSpec((tm, tn), lambda j, i: (i, j)),
        out_shape=jax.ShapeDtypeStruct((rows, d), MXU_DTYPE),
        compiler_params=_params(("arbitrary", "arbitrary"), blk),
        name="merge",
    )(a, cv, w_ao, w_co, proj, proj)


def _out_proj_kernel(z_ref, w_ref, x_ref, g_ref, o_ref):
    m = jnp.dot(z_ref[...], _bf16(w_ref[...]), preferred_element_type=jnp.float32)
    for r in range(0, m.shape[0], V7X_SUBLANES):
        gi = r // V7X_SUBLANES
        rs = slice(r, r + V7X_SUBLANES)
        o_ref[rs, :] = x_ref[rs, :] + g_ref[gi:gi + 1, :] * m[rs]


def _out_proj(z, w_out, x, modg, layer, *, gate_col):
    rows, d = x.shape
    tm = _pick_tile(rows, 768, _BF16_ROWS)
    tn = _pick_tile(d, 512, V7X_LANES)
    gcol = gate_col * (d // tn)
    blk = tm * d * 2 + d * tn * 6 + tm * tn * 9
    return pl.pallas_call(
        _out_proj_kernel,
        grid=(d // tn, rows // tm),
        in_specs=[pl.BlockSpec((tm, d), lambda j, i: (i, 0)),
                  pl.BlockSpec((None, d, tn), lambda j, i: (layer, 0, j)),
                  pl.BlockSpec((tm, tn), lambda j, i: (i, j)),
                  pl.BlockSpec((tm // V7X_SUBLANES, tn), lambda j, i: (i, gcol + j))],
        out_specs=pl.BlockSpec((tm, tn), lambda j, i: (i, j)),
        out_shape=jax.ShapeDtypeStruct((rows, d), jnp.float32),
        compiler_params=_params(("arbitrary", "arbitrary"), blk),
        name="out_proj",
    )(z, w_out, x, modg)


def _first_argmax(x, axis):
    m = jnp.max(x, axis=axis, keepdims=True)
    idx = lax.broadcasted_iota(jnp.int32, x.shape, axis)
    first = jnp.min(jnp.where(x == m, idx, x.shape[axis]), axis=axis, keepdims=True)
    return m, first


def _router_kernel(x_ref, g_ref, sc_ref, sh_ref, rwt_ref, rb_ref, h_ref, eid_ref, wt_ref, rank_ref, cnt_ref,
                   *, n_experts):
    tr = x_ref.shape[0]
    for r in range(0, tr, V7X_SUBLANES):
        h_ref[r:r + V7X_SUBLANES, :] = _modulated_norm_rows(x_ref, g_ref, sc_ref, sh_ref, r, V7X_SUBLANES)
    per_group = n_experts // N_GROUPS
    logits = lax.dot_general(rwt_ref[...], h_ref[...], (((1,), (1,)), ((), ())),
                             precision=lax.Precision.HIGHEST,
                             preferred_element_type=jnp.float32)
    s = jax.nn.sigmoid(logits)
    sel = (s + rb_ref[...]).reshape(N_GROUPS, per_group, tr)
    m1, i1 = _first_argmax(sel, 1)
    e_iota = lax.broadcasted_iota(jnp.int32, sel.shape, 1)
    m2 = jnp.max(jnp.where(e_iota == i1, -jnp.inf, sel), axis=1, keepdims=True)
    _, grp = _first_argmax(m1 + m2, 0)
    g_iota = lax.broadcasted_iota(jnp.int32, sel.shape, 0)
    in_group = jnp.max(jnp.where(g_iota == grp, sel, -jnp.inf), axis=0, keepdims=True)
    _, j1 = _first_argmax(in_group, 1)
    j_iota = lax.broadcasted_iota(jnp.int32, in_group.shape, 1)
    _, j2 = _first_argmax(jnp.where(j_iota == j1, -jnp.inf, in_group), 1)
    e1 = (grp * per_group + j1).reshape(1, tr)
    e2 = (grp * per_group + j2).reshape(1, tr)
    x_iota = lax.broadcasted_iota(jnp.int32, s.shape, 0)
    hit1 = x_iota == e1
    hit2 = x_iota == e2
    w1 = jnp.sum(jnp.where(hit1, s, 0.0), axis=0, keepdims=True)
    w2 = jnp.sum(jnp.where(hit2, s, 0.0), axis=0, keepdims=True)
    tot = w1 + w2
    eid_ref[...] = jnp.concatenate([e1, e2], axis=0)
    wt_ref[...] = jnp.concatenate([w1 / tot, w2 / tot], axis=0)

    @pl.when(pl.program_id(0) == 0)
    def _():
        cnt_ref[...] = jnp.zeros(cnt_ref.shape, jnp.float32)

    one1 = jnp.where(hit1, 1.0, 0.0)
    one2 = jnp.where(hit2, 1.0, 0.0)
    before = (lax.broadcasted_iota(jnp.int32, (tr, tr), 0) <
              lax.broadcasted_iota(jnp.int32, (tr, tr), 1)).astype(MXU_DTYPE)
    pre1 = jnp.dot(_bf16(one1), before, preferred_element_type=jnp.float32)
    pre2 = jnp.dot(_bf16(one2), before, preferred_element_type=jnp.float32)
    tot1 = jnp.sum(one1, axis=1, keepdims=True)
    tot2 = jnp.sum(one2, axis=1, keepdims=True)
    base = cnt_ref[:, 0:1]
    r1 = jnp.sum(one1 * (base + pre1), axis=0, keepdims=True)
    r2 = jnp.sum(one2 * (base + tot1 + pre2), axis=0, keepdims=True)
    rank_ref[...] = jnp.concatenate([r1, r2], axis=0).astype(jnp.int32)
    cnt_ref[...] = cnt_ref[...] + (tot1 + tot2)


def _norm_router(x, g, modg, router_w, router_b, *, shift_col, scale_col):
    rows, d = x.shape
    n_experts = router_w.shape[1]
    tr = _pick_tile(rows, 256, V7X_LANES) if rows % V7X_LANES == 0 else rows
    tg = tr // V7X_SUBLANES
    kern = functools.partial(_router_kernel, n_experts=n_experts)
    return pl.pallas_call(
        kern,
        grid=(rows // tr,),
        in_specs=[pl.BlockSpec((tr, d), lambda i: (i, 0)),
                  pl.BlockSpec((1, d), lambda i: (0, 0)),
                  pl.BlockSpec((tg, d), lambda i: (i, scale_col)),
                  pl.BlockSpec((tg, d), lambda i: (i, shift_col)),
                  pl.BlockSpec((n_experts, d), lambda i: (0, 0)),
                  pl.BlockSpec((n_experts, 1), lambda i: (0, 0))],
        out_specs=[pl.BlockSpec((tr, d), lambda i: (i, 0)),
                   pl.BlockSpec((TOP_K, tr), lambda i: (0, i)),
                   pl.BlockSpec((TOP_K, tr), lambda i: (0, i)),
                   pl.BlockSpec((TOP_K, tr), lambda i: (0, i)),
                   pl.BlockSpec((n_experts, V7X_LANES), lambda i: (0, 0))],
        out_shape=[jax.ShapeDtypeStruct((rows, d), jnp.float32),
                   jax.ShapeDtypeStruct((TOP_K, rows), jnp.int32),
                   jax.ShapeDtypeStruct((TOP_K, rows), jnp.float32),
                   jax.ShapeDtypeStruct((TOP_K, rows), jnp.int32),
                   jax.ShapeDtypeStruct((n_experts, V7X_LANES), jnp.float32)],
        compiler_params=_params(("arbitrary",), tr * d * 16 + n_experts * d * 4),
        name="norm_router",
    )(x, g.reshape(1, d), modg, modg, router_w.T, router_b.reshape(n_experts, 1))


def _dispatch_plan(eid, rank, counts, block_rows):
    top_k, rows = eid.shape
    n_experts = counts.shape[0]
    n_assign = top_k * rows
    n_blocks = -(-(n_assign + n_experts * (block_rows - 1)) // block_rows)
    e_flat = eid.reshape(n_assign)
    padded = (counts + block_rows - 1) // block_rows * block_rows
    pend = jnp.cumsum(padded)
    poff = pend - padded
    dest = poff[e_flat] + rank.reshape(n_assign)
    tok = jnp.tile(jnp.arange(rows, dtype=jnp.int32), top_k)
    row_tok = jnp.zeros((n_blocks * block_rows,), jnp.int32).at[dest].set(tok)
    n_used = pend[-1] // block_rows
    b_all = jnp.arange(n_blocks, dtype=jnp.int32)
    blk = jnp.minimum(b_all, n_used - 1)
    blk_exp = jnp.minimum(jnp.searchsorted(pend, blk * block_rows, side='right'), n_experts - 1)
    filled = jnp.clip(poff[blk_exp] + counts[blk_exp] - b_all * block_rows, 0, block_rows)
    blk_cnt = jnp.where(b_all < n_used, filled, 0)
    return (row_tok.reshape(n_blocks, 1, block_rows), dest.reshape(top_k, rows).astype(jnp.int32),
            blk_exp.astype(jnp.int32), blk_cnt.astype(jnp.int32), n_used.reshape(1).astype(jnp.int32))


def _experts_kernel(exp_ref, cnt_ref, nused_ref, tok0_ref, tok_next_ref, h_ref, wg_hbm, wu_hbm, wd_hbm,
                    o_ref, x_buf, sems, w_buf, w_sems, g_acc, u_acc, h_buf, *, layer):
    b = pl.program_id(0)
    n_used = nused_ref[0]
    used = b < n_used
    slot = b % 2
    p = MOE_PHASE_STEPS
    n_chunks = 3 * p
    d_e = w_buf.shape[1]

    def chunk_copy(blk, c):
        e = exp_ref[blk]
        if c < 2 * p:
            src = (wg_hbm if c % 2 == 0 else wu_hbm).at[layer, e, pl.ds((c // 2) * d_e, d_e), :]
        else:
            src = wd_hbm.at[layer, e, :, pl.ds((c - 2 * p) * d_e, d_e)]
        ring = c % MOE_WEIGHT_SLOTS
        return pltpu.make_async_copy(src, w_buf.at[ring], w_sems.at[ring])

    def row_copy(tok_ref, r, slot_):
        return pltpu.make_async_copy(h_ref.at[pl.ds(tok_ref[0, r], 1)], x_buf.at[slot_, pl.ds(r, 1)],
                                     sems.at[slot_])

    def gather_start(tok_ref, n, slot_):
        def body(grp, c):
            for u in range(DMA_ISSUE_UNROLL):
                row_copy(tok_ref, grp * DMA_ISSUE_UNROLL + u, slot_).start()
            return c

        lax.fori_loop(0, lax.shift_right_logical(n + (DMA_ISSUE_UNROLL - 1), DMA_ISSUE_SHIFT), body, 0)

    def gather_wait(tok_ref, n, slot_):
        def body(grp, c):
            for u in range(DMA_ISSUE_UNROLL):
                row_copy(tok_ref, grp * DMA_ISSUE_UNROLL + u, slot_).wait()
            return c

        lax.fori_loop(0, lax.shift_right_logical(n + (DMA_ISSUE_UNROLL - 1), DMA_ISSUE_SHIFT), body, 0)

    @pl.when(b == 0)
    def _():
        for c in range(MOE_WEIGHT_LOOKAHEAD):
            chunk_copy(0, c).start()
        x_buf[...] = jnp.zeros(x_buf.shape, x_buf.dtype)
        gather_start(tok0_ref, cnt_ref[0], 0)

    @pl.when(used)
    def _():
        gather_wait(tok0_ref, cnt_ref[b], slot)

        @pl.when(b + 1 < n_used)
        def _():
            gather_start(tok_next_ref, cnt_ref[b + 1], 1 - slot)

        for c in range(n_chunks):
            chunk_copy(b, c).wait()
            ahead = c + MOE_WEIGHT_LOOKAHEAD
            if ahead < n_chunks:
                chunk_copy(b, ahead).start()
            else:
                @pl.when(b + 1 < n_used)
                def _(ahead=ahead):
                    chunk_copy(b + 1, ahead - n_chunks).start()

            w = _bf16(w_buf[c % MOE_WEIGHT_SLOTS])
            if c < 2 * p:
                k = c // 2
                x = _bf16(x_buf[slot, :, k * d_e:(k + 1) * d_e])
                part = jnp.dot(x, w, preferred_element_type=jnp.float32)
                acc = g_acc if c % 2 == 0 else u_acc
                if k == 0:
                    acc[...] = part
                else:
                    acc[...] += part
                if c == 2 * p - 1:
                    g = g_acc[...]
                    h_buf[...] = _bf16(g * jax.nn.sigmoid(g) * u_acc[...])
            else:
                n = c - 2 * p
                o_ref[:, n * d_e:(n + 1) * d_e] = jnp.dot(h_buf[...], w, preferred_element_type=jnp.float32)


def _experts(h, row_tok, w_gate, w_up, w_down, layer, blk_exp, blk_cnt, n_used):
    d = h.shape[1]
    d_e = w_gate.shape[3]
    n_blocks, _, block_rows = row_tok.shape
    n_rows = n_blocks * block_rows
    if d_e * MOE_PHASE_STEPS != d or (3 * MOE_PHASE_STEPS) % MOE_WEIGHT_SLOTS:
        raise ValueError("expert weight chunks must be square and fill the ring a whole number of times")
    if block_rows % DMA_ISSUE_UNROLL:
        raise ValueError("expert block rows must be a whole number of DMA issue groups")

    def out_map(b, exp, cnt, nu):
        return (jnp.minimum(b, nu[0] - 1), 0)

    def tok_next_map(b, exp, cnt, nu):
        return (jnp.minimum(b + 1, nu[0] - 1), 0, 0)

    grid_spec = pltpu.PrefetchScalarGridSpec(
        num_scalar_prefetch=3,
        grid=(n_blocks,),
        in_specs=[pl.BlockSpec((None, 1, block_rows), lambda b, exp, cnt, nu: (0, 0, 0), memory_space=pltpu.SMEM),
                  pl.BlockSpec((None, 1, block_rows), tok_next_map, memory_space=pltpu.SMEM),
                  pl.BlockSpec(memory_space=pl.ANY),
                  pl.BlockSpec(memory_space=pl.ANY),
                  pl.BlockSpec(memory_space=pl.ANY),
                  pl.BlockSpec(memory_space=pl.ANY)],
        out_specs=pl.BlockSpec((block_rows, d), out_map),
        scratch_shapes=[pltpu.VMEM((2, block_rows, d), jnp.float32),
                        pltpu.SemaphoreType.DMA((2,)),
                        pltpu.VMEM((MOE_WEIGHT_SLOTS, d_e, d_e), jnp.float32),
                        pltpu.SemaphoreType.DMA((MOE_WEIGHT_SLOTS,)),
                        pltpu.VMEM((block_rows, d_e), jnp.float32),
                        pltpu.VMEM((block_rows, d_e), jnp.float32),
                        pltpu.VMEM((block_rows, d_e), MXU_DTYPE)],
    )
    blk_bytes = (block_rows * d * 4 * 2 + MOE_WEIGHT_SLOTS * d_e * d_e * 2 + d_e * d_e * 2 + block_rows * d_e * 5)
    return pl.pallas_call(
        functools.partial(_experts_kernel, layer=layer),
        grid_spec=grid_spec,
        out_shape=jax.ShapeDtypeStruct((n_rows, d), jnp.float32),
        compiler_params=_params(("arbitrary",), blk_bytes),
        name="moe_experts",
    )(blk_exp, blk_cnt, n_used, row_tok, row_tok, h, w_gate, w_up, w_down)


def _combine_kernel(dest_ref, y_ref, x_ref, wt_ref, g_ref, *refs, tr, total_rows, prompt_steps):
    out_refs, (ybuf, sems) = refs[:-2], refs[-2:]
    i = pl.program_id(0)
    slot = i % 2

    def copy(step, slot_, k, r):
        row = dest_ref[k * total_rows + step * tr + r]
        return pltpu.make_async_copy(y_ref.at[pl.ds(row, 1)], ybuf.at[slot_, k, pl.ds(r, 1)], sems.at[slot_])

    def gather_start(step, slot_):
        def body(grp, c):
            for u in range(DMA_ISSUE_UNROLL):
                for k in range(TOP_K):
                    copy(step, slot_, k, grp * DMA_ISSUE_UNROLL + u).start()
            return c

        lax.fori_loop(0, tr // DMA_ISSUE_UNROLL, body, 0)

    @pl.when(i == 0)
    def _():
        gather_start(0, 0)

    @pl.when(i + 1 < pl.num_programs(0))
    def _():
        gather_start(i + 1, 1 - slot)

    def wait(grp, c):
        for u in range(DMA_ISSUE_UNROLL):
            for k in range(TOP_K):
                copy(i, slot, k, grp * DMA_ISSUE_UNROLL + u).wait()
        return c

    lax.fori_loop(0, tr // DMA_ISSUE_UNROLL, wait, 0)

    def emit(o_ref):
        for r in range(0, tr, V7X_SUBLANES):
            gi = r // V7X_SUBLANES
            rs = slice(r, r + V7X_SUBLANES)
            y = wt_ref[rs, 0:1] * ybuf[slot, 0, rs, :] + wt_ref[rs, 1:2] * ybuf[slot, 1, rs, :]
            o_ref[rs, :] = x_ref[rs, :] + g_ref[gi:gi + 1, :] * y

    if prompt_steps is None:
        emit(out_refs[0])
    else:
        pl.when(i < prompt_steps)(lambda: emit(out_refs[0]))
        pl.when(i >= prompt_steps)(lambda: emit(out_refs[1]))


def _combine(yb, dest, wts, x, modg, *, gate_col, split_rows=None):
    rows, d = x.shape
    if split_rows is None:
        tr = _pick_tile(rows, 128, V7X_SUBLANES)
        p_steps = None
        out_specs = pl.BlockSpec((tr, d), lambda i, dst: (i, 0))
        out_shape = jax.ShapeDtypeStruct((rows, d), jnp.float32)
    else:
        tr = _pick_tile(math.gcd(split_rows, rows - split_rows), 128, V7X_SUBLANES)
        p_steps = split_rows // tr
        out_specs = [pl.BlockSpec((tr, d), lambda i, dst: (jnp.minimum(i, p_steps - 1), 0)),
                     pl.BlockSpec((tr, d), lambda i, dst: (jnp.maximum(i - p_steps, 0), 0))]
        out_shape = [jax.ShapeDtypeStruct((split_rows, d), jnp.float32),
                     jax.ShapeDtypeStruct((rows - split_rows, d), jnp.float32)]
    kern = functools.partial(_combine_kernel, tr=tr, total_rows=rows, prompt_steps=p_steps)
    grid_spec = pltpu.PrefetchScalarGridSpec(
        num_scalar_prefetch=1,
        grid=(rows // tr,),
        in_specs=[pl.BlockSpec(memory_space=pl.ANY),
                  pl.BlockSpec((tr, d), lambda i, dst: (i, 0)),
                  pl.BlockSpec((tr, TOP_K), lambda i, dst: (i, 0)),
                  pl.BlockSpec((tr // V7X_SUBLANES, d), lambda i, dst: (i, gate_col))],
        out_specs=out_specs,
        scratch_shapes=[pltpu.VMEM((2, TOP_K, tr, d), yb.dtype),
                        pltpu.SemaphoreType.DMA((2,))],
    )
    return pl.pallas_call(
        kern,
        grid_spec=grid_spec,
        out_shape=out_shape,
        compiler_params=_params(("arbitrary",), tr * d * 20),
        name="moe_combine",
    )(dest.reshape(-1), yb, x, wts.T, modg)


def kernel(x_prompt, x_sample, cache_k, cache_v, state_conv, c_prompt, c_sample, router_w, router_b, norm1_g, norm2_g, w_ada, b_ada, w_in, q_norm_g, k_norm_g, attn_sink, w_ao, dw_w, dw_b, cn_g, cn_b, w_co, w_out, moe_w_gate, moe_w_up, moe_w_down):
    batch, seq, d = x_prompt.shape
    dec_batch, t_new, _ = x_sample.shape
    depth = w_in.shape[0]
    window = cache_k.shape[2]
    n_kv = cache_k.shape[3]
    d_kv = n_kv * cache_k.shape[4]
    d_q = w_ao.shape[1]
    d_conv = w_co.shape[1]
    n_experts = router_w.shape[1]
    if q_norm_g.shape[1] != HEAD_DIM or cache_k.shape[4] != HEAD_DIM or dw_w.shape[1] != CONV_WIDTH:
        raise ValueError("unsupported head / conv geometry")
    if d_q != Q_PER_KV * d_kv or n_experts % N_GROUPS:
        raise ValueError("unsupported head / expert grouping")
    if seq % window or t_new % V7X_SUBLANES or t_new > window:
        raise ValueError("unsupported sequence geometry")
    prompt_rows = batch * seq
    sample_rows = dec_batch * t_new
    rows = prompt_rows + sample_rows
    ga_off = d_q + 2 * d_kv + 2 * d_conv

    SH1, SC1, G1, SH2, SC2, G2 = range(6)

    kp, vp, up, ks, vs, us = [], [], [], [], [], []
    for l in range(depth):
        mg = _adaln(c_prompt, c_sample, w_ada, b_ada, l, prompt_groups=seq // V7X_SUBLANES,
                    sample_groups=t_new // V7X_SUBLANES)
        if l == 0:
            h, x = _norm_mod_stack(x_prompt.reshape(prompt_rows, d), x_sample.reshape(sample_rows, d),
                                   norm1_g[l], mg, SH1, SC1)
        else:
            h = _norm_mod(x, norm1_g[l], mg, SH1, SC1)
        proj = _in_proj(h, w_in, l)

        a, k_last, v_last = _attn_prompt(proj, attn_sink[l], q_norm_g[l], k_norm_g[l], batch=batch, seq=seq,
                                         total_rows=rows, d_q=d_q, d_kv=d_kv, window=window)
        a, k_buf, v_buf = _attn_sample(a, proj, cache_k, cache_v, l, attn_sink[l], q_norm_g[l], k_norm_g[l],
                                       prompt_rows=prompt_rows, dec_batch=dec_batch, t_new=t_new,
                                       d_q=d_q, d_kv=d_kv, window=window)
        cv, u_last = _conv_prompt(proj, dw_w[l], dw_b[l], cn_g[l], cn_b[l], batch=batch, seq=seq,
                                  total_rows=rows, d_q=d_q, d_kv=d_kv, d_conv=d_conv)
        cv, u_new = _conv_sample(cv, proj, state_conv, l, dw_w[l], dw_b[l], cn_g[l], cn_b[l],
                                 prompt_rows=prompt_rows, dec_batch=dec_batch, t_new=t_new,
                                 d_q=d_q, d_kv=d_kv, d_conv=d_conv)

        z = _merge(a, cv, proj, w_ao, w_co, l, ga_off=ga_off)
        x = _out_proj(z, w_out, x, mg, l, gate_col=G1)

        h2, eid, wts, rank, counts = _norm_router(x, norm2_g[l], mg, router_w, router_b,
                                                  shift_col=SH2, scale_col=SC2)
        row_tok, dest, blk_exp, blk_cnt, n_used = _dispatch_plan(eid, rank, counts[:, 0].astype(jnp.int32),
                                                                 MOE_BLOCK_ROWS)
        yb = _experts(h2, row_tok, moe_w_gate, moe_w_up, moe_w_down, l, blk_exp, blk_cnt, n_used)
        if l + 1 < depth:
            x = _combine(yb, dest, wts, x, mg, gate_col=G2)
        else:
            y_prompt, y_sample = _combine(yb, dest, wts, x, mg, gate_col=G2, split_rows=prompt_rows)

        kp.append(k_last)
        vp.append(v_last)
        up.append(u_last)
        ks.append(k_buf)
        vs.append(v_buf)
        us.append(u_new)

    def heads(bufs, n_seq):
        return jnp.stack(bufs).reshape(depth, n_seq, window, n_kv, HEAD_DIM)

    y_prompt = y_prompt.reshape(batch, seq, d)
    y_sample = y_sample.reshape(dec_batch, t_new, d)
    return (y_prompt, y_sample, heads(kp, batch), heads(vp, batch), jnp.stack(up),
            heads(ks, dec_batch), heads(vs, dec_batch), jnp.stack(us))
```

```python
import functools
import math

import jax
import jax.numpy as jnp
from jax import lax
from jax.experimental import pallas as pl
from jax.experimental.pallas import tpu as pltpu

EPS = 1e-6
HEAD_DIM = 128
Q_PER_KV = 4
CONV_WIDTH = 31
N_GROUPS = 8
TOP_K = 2

V7X_SUBLANES = 8
V7X_LANES = 128
V7X_VMEM_BYTES = 64 * 1024 * 1024
_MIB = 1024 * 1024
_BF16_ROWS = 2 * V7X_SUBLANES
MXU_DTYPE = jnp.bfloat16

CONV_HALO = 32
CONV_LEAD = CONV_HALO - (CONV_WIDTH - 1)
CONV_ROW_CHUNK = 64
CONV_CH_CHUNK = 256
MOE_BLOCK_ROWS = 384
MOE_PHASE_STEPS = 4
MOE_WEIGHT_SLOTS = 4
MOE_WEIGHT_LOOKAHEAD = MOE_WEIGHT_SLOTS - 1
DMA_ISSUE_SHIFT = 3
DMA_ISSUE_UNROLL = 1 << DMA_ISSUE_SHIFT


def _bf16(x):
    return x.astype(MXU_DTYPE)


def _pick_tile(n, target, align):
    best = None
    for t in range(align, min(n, target) + 1, align):
        if n % t == 0:
            best = t
    if best is None:
        raise ValueError(f"no tile for n={n} target={target} align={align}")
    return best


def _exact_div(a, b):
    q, r = divmod(a, b)
    if r:
        raise ValueError(f"{a} is not a multiple of {b}")
    return q


def _params(semantics, block_bytes):
    need = 2 * block_bytes + 8 * _MIB
    limit = int(min(max(need, 16 * _MIB), V7X_VMEM_BYTES - 8 * _MIB))
    return pltpu.CompilerParams(dimension_semantics=semantics, vmem_limit_bytes=limit)


def _adaln_kernel(c_ref, w_ref, b_ref, o_ref, *, batch, sample_row0, dec_batch, prompt_groups, sample_groups):
    c = c_ref[...]
    a = _bf16(c * jax.nn.sigmoid(c))
    mod = jnp.dot(a, _bf16(w_ref[...]), preferred_element_type=jnp.float32) + b_ref[...]
    tn = mod.shape[1]
    for i in range(batch):
        o_ref[i * prompt_groups:(i + 1) * prompt_groups, :] = jnp.broadcast_to(mod[i:i + 1, :], (prompt_groups, tn))
    base = batch * prompt_groups
    if sample_groups == 1:
        o_ref[base:base + dec_batch, :] = mod[sample_row0:sample_row0 + dec_batch, :]
    else:
        for j in range(dec_batch):
            o_ref[base + j * sample_groups:base + (j + 1) * sample_groups, :] = jnp.broadcast_to(
                mod[sample_row0 + j:sample_row0 + j + 1, :], (sample_groups, tn))


def _adaln(c_prompt, c_sample, w_ada, b_ada, layer, *, prompt_groups, sample_groups):
    depth, d, n = w_ada.shape
    batch, dec_batch = c_prompt.shape[0], c_sample.shape[0]
    sample_row0 = -(-batch // V7X_SUBLANES) * V7X_SUBLANES
    c_all = jnp.concatenate([c_prompt, jnp.zeros((sample_row0 - batch, d), jnp.float32), c_sample,
                             jnp.zeros((-dec_batch % V7X_SUBLANES, d), jnp.float32)], axis=0)
    rows = c_all.shape[0]
    groups = batch * prompt_groups + dec_batch * sample_groups
    tn = _pick_tile(n, 1024, V7X_LANES)
    kern = functools.partial(_adaln_kernel, batch=batch, sample_row0=sample_row0, dec_batch=dec_batch,
                             prompt_groups=prompt_groups, sample_groups=sample_groups)
    return pl.pallas_call(
        kern,
        grid=(n // tn,),
        in_specs=[pl.BlockSpec((rows, d), lambda j: (0, 0)),
                  pl.BlockSpec((None, d, tn), lambda j: (layer, 0, j)),
                  pl.BlockSpec((None, 1, tn), lambda j: (layer, 0, j))],
        out_specs=pl.BlockSpec((groups, tn), lambda j: (0, j)),
        out_shape=jax.ShapeDtypeStruct((groups, n), jnp.float32),
        compiler_params=_params(("arbitrary",), d * tn * 6 + rows * d * 4 + groups * tn * 4),
        name="adaln",
    )(c_all, w_ada, b_ada.reshape(depth, 1, n))


def _modulated_norm_rows(x_ref, g_ref, sc_ref, sh_ref, row0, n_rows):
    g = g_ref[...]
    parts = []
    for r in range(row0, row0 + n_rows, V7X_SUBLANES):
        x = x_ref[r:r + V7X_SUBLANES, :]
        gi = r // V7X_SUBLANES
        y = x * lax.rsqrt(jnp.mean(x * x, axis=-1, keepdims=True) + EPS) * g
        parts.append(y * (1.0 + sc_ref[gi:gi + 1, :]) + sh_ref[gi:gi + 1, :])
    return parts[0] if len(parts) == 1 else jnp.concatenate(parts, axis=0)


def _norm_mod_kernel(x_ref, g_ref, sc_ref, sh_ref, o_ref):
    for r in range(0, o_ref.shape[0], _BF16_ROWS):
        o_ref[r:r + _BF16_ROWS, :] = _bf16(_modulated_norm_rows(x_ref, g_ref, sc_ref, sh_ref, r, _BF16_ROWS))


def _norm_mod(x, g, modg, shift_col, scale_col):
    rows, d = x.shape
    tr = _pick_tile(rows, 256, _BF16_ROWS)
    tg = tr // V7X_SUBLANES
    return pl.pallas_call(
        _norm_mod_kernel,
        grid=(rows // tr,),
        in_specs=[pl.BlockSpec((tr, d), lambda i: (i, 0)),
                  pl.BlockSpec((1, d), lambda i: (0, 0)),
                  pl.BlockSpec((tg, d), lambda i: (i, scale_col)),
                  pl.BlockSpec((tg, d), lambda i: (i, shift_col))],
        out_specs=pl.BlockSpec((tr, d), lambda i: (i, 0)),
        out_shape=jax.ShapeDtypeStruct((rows, d), MXU_DTYPE),
        compiler_params=_params(("arbitrary",), tr * d * 8),
        name="norm_mod",
    )(x, g.reshape(1, d), modg, modg)


def _norm_mod_stack_kernel(xp_ref, xs_ref, g_ref, sc_ref, sh_ref, o_ref, x_ref, *, prompt_steps):
    def emit(src_ref):
        x_ref[...] = src_ref[...]
        _norm_mod_kernel(src_ref, g_ref, sc_ref, sh_ref, o_ref)

    is_prompt = pl.program_id(0) < prompt_steps
    pl.when(is_prompt)(lambda: emit(xp_ref))
    pl.when(jnp.logical_not(is_prompt))(lambda: emit(xs_ref))


def _norm_mod_stack(x_prompt, x_sample, g, modg, shift_col, scale_col):
    p_rows, d = x_prompt.shape
    s_rows = x_sample.shape[0]
    rows = p_rows + s_rows
    tr = _pick_tile(math.gcd(p_rows, s_rows), 256, _BF16_ROWS)
    tg = tr // V7X_SUBLANES
    p_steps = p_rows // tr
    kern = functools.partial(_norm_mod_stack_kernel, prompt_steps=p_steps)
    return pl.pallas_call(
        kern,
        grid=(rows // tr,),
        in_specs=[pl.BlockSpec((tr, d), lambda i: (jnp.minimum(i, p_steps - 1), 0)),
                  pl.BlockSpec((tr, d), lambda i: (jnp.maximum(i - p_steps, 0), 0)),
                  pl.BlockSpec((1, d), lambda i: (0, 0)),
                  pl.BlockSpec((tg, d), lambda i: (i, scale_col)),
                  pl.BlockSpec((tg, d), lambda i: (i, shift_col))],
        out_specs=[pl.BlockSpec((tr, d), lambda i: (i, 0)),
                   pl.BlockSpec((tr, d), lambda i: (i, 0))],
        out_shape=[jax.ShapeDtypeStruct((rows, d), MXU_DTYPE),
                   jax.ShapeDtypeStruct((rows, d), jnp.float32)],
        compiler_params=_params(("arbitrary",), tr * d * 16),
        name="norm_mod_stack",
    )(x_prompt, x_sample, g.reshape(1, d), modg, modg)


def _matmul_kernel(x_ref, w_ref, o_ref):
    o_ref[...] = jnp.dot(x_ref[...], _bf16(w_ref[...]), preferred_element_type=jnp.float32)


def _in_proj(h, w_in, layer):
    rows, k = h.shape
    n = w_in.shape[2]
    tm = _pick_tile(rows, 1408, _BF16_ROWS)
    tn = _pick_tile(n, 512, V7X_LANES)
    return pl.pallas_call(
        _matmul_kernel,
        grid=(n // tn, rows // tm),
        in_specs=[pl.BlockSpec((tm, k), lambda j, i: (i, 0)),
                  pl.BlockSpec((None, k, tn), lambda j, i: (layer, 0, j))],
        out_specs=pl.BlockSpec((tm, tn), lambda j, i: (i, j)),
        out_shape=jax.ShapeDtypeStruct((rows, n), jnp.float32),
        compiler_params=_params(("arbitrary", "arbitrary"), tm * k * 2 + k * tn * 6 + tm * tn * 4),
        name="in_proj",
    )(h, w_in)


def _head_rms(x, g):
    return x * lax.rsqrt(jnp.mean(x * x, axis=-1, keepdims=True) + EPS) * g


def _window_mask(n_q, window, older_present):
    n_rows = Q_PER_KV * n_q
    row = lax.broadcasted_iota(jnp.int32, (n_rows, 2 * window), 0) % n_q
    col = lax.broadcasted_iota(jnp.int32, (n_rows, 2 * window), 1)
    return (col >= row) & (col <= row + window) & ((col >= window) | older_present)


def _kv_group_attention(q_heads, k_all, v_all, sink_vals, valid, n_q):
    q_all = jnp.concatenate(q_heads, axis=0)
    sink_col = jnp.concatenate([jnp.full((n_q, 1), sv, jnp.float32) for sv in sink_vals], axis=0)
    s = lax.dot_general(_bf16(q_all), _bf16(k_all), (((1,), (1,)), ((), ())),
                        preferred_element_type=jnp.float32) * (1.0 / math.sqrt(HEAD_DIM))
    s = jnp.where(valid, s, -jnp.inf)
    m = jnp.maximum(jnp.max(s, axis=-1, keepdims=True), sink_col)
    p = jnp.exp(s - m)
    denom = jnp.sum(p, axis=-1, keepdims=True) + jnp.exp(sink_col - m)
    return jnp.dot(_bf16(p), _bf16(v_all), preferred_element_type=jnp.float32) / denom


def _attn_prompt_kernel(sink_ref, q_ref, kp_ref, kc_ref, vp_ref, vc_ref, qg_ref, kg_ref, o_ref, kn_ref,
                        vn_ref, *, n_kv, window):
    qg = qg_ref[...]
    kg = kg_ref[...]
    valid = _window_mask(window, window, pl.program_id(1) > 0)
    vn_ref[...] = vc_ref[...]
    for h in range(n_kv):
        ksl = slice(h * HEAD_DIM, (h + 1) * HEAD_DIM)
        k_cur = _head_rms(kc_ref[:, ksl], kg)
        kn_ref[:, ksl] = k_cur
        k_all = jnp.concatenate([_head_rms(kp_ref[:, ksl], kg), k_cur], axis=0)
        v_all = jnp.concatenate([vp_ref[:, ksl], vc_ref[:, ksl]], axis=0)
        heads = [h * Q_PER_KV + g for g in range(Q_PER_KV)]
        q_heads = [_head_rms(q_ref[:, hq * HEAD_DIM:(hq + 1) * HEAD_DIM], qg) for hq in heads]
        o = _kv_group_attention(q_heads, k_all, v_all, [sink_ref[hq] for hq in heads], valid, window)
        for g, hq in enumerate(heads):
            o_ref[:, hq * HEAD_DIM:(hq + 1) * HEAD_DIM] = _bf16(o[g * window:(g + 1) * window])


def _attn_prompt(proj, sink, q_g, k_g, *, batch, seq, total_rows, d_q, d_kv, window):
    n_kv = d_kv // HEAD_DIM
    nb = _exact_div(seq, window)
    k_col = _exact_div(d_q, d_kv)
    v_col = k_col + 1
    kern = functools.partial(_attn_prompt_kernel, n_kv=n_kv, window=window)

    def cur(b, i, s):
        return b * nb + i

    def prev(b, i, s):
        return b * nb + jnp.maximum(i - 1, 0)

    grid_spec = pltpu.PrefetchScalarGridSpec(
        num_scalar_prefetch=1,
        grid=(batch, nb),
        in_specs=[pl.BlockSpec((window, d_q), lambda b, i, s: (cur(b, i, s), 0)),
                  pl.BlockSpec((window, d_kv), lambda b, i, s: (prev(b, i, s), k_col)),
                  pl.BlockSpec((window, d_kv), lambda b, i, s: (cur(b, i, s), k_col)),
                  pl.BlockSpec((window, d_kv), lambda b, i, s: (prev(b, i, s), v_col)),
                  pl.BlockSpec((window, d_kv), lambda b, i, s: (cur(b, i, s), v_col)),
                  pl.BlockSpec((1, HEAD_DIM), lambda b, i, s: (0, 0)),
                  pl.BlockSpec((1, HEAD_DIM), lambda b, i, s: (0, 0))],
        out_specs=[pl.BlockSpec((window, d_q), lambda b, i, s: (cur(b, i, s), 0)),
                   pl.BlockSpec((None, window, d_kv), lambda b, i, s: (b, 0, 0)),
                   pl.BlockSpec((None, window, d_kv), lambda b, i, s: (b, 0, 0))],
    )
    return pl.pallas_call(
        kern,
        grid_spec=grid_spec,
        out_shape=[jax.ShapeDtypeStruct((total_rows, d_q), MXU_DTYPE),
                   jax.ShapeDtypeStruct((batch, window, d_kv), jnp.float32),
                   jax.ShapeDtypeStruct((batch, window, d_kv), jnp.float32)],
        compiler_params=_params(("arbitrary", "arbitrary"), window * (d_q * 6 + d_kv * 28)),
        name="attn_prompt",
    )(sink, proj, proj, proj, proj, proj, q_g.reshape(1, HEAD_DIM), k_g.reshape(1, HEAD_DIM))


def _attn_sample_kernel(sink_ref, a_in_ref, q_ref, kn_ref, vn_ref, kc_ref, vc_ref, qg_ref, kg_ref,
                        o_ref, kout_ref, vout_ref, *, n_kv, window, t_new, seqs):
    del a_in_ref
    qg = qg_ref[...]
    kg = kg_ref[...]
    valid = _window_mask(t_new, window, True)
    fill = jnp.zeros((window - t_new, HEAD_DIM), jnp.float32)
    keep = window - t_new
    for sq in range(seqs):
        rows = slice(sq * t_new, (sq + 1) * t_new)
        kout_ref[sq, 0:keep, :] = kc_ref[sq, t_new:window, :]
        vout_ref[sq, 0:keep, :] = vc_ref[sq, t_new:window, :]
        vout_ref[sq, keep:window, :] = vn_ref[rows, :]
        for h in range(n_kv):
            ksl = slice(h * HEAD_DIM, (h + 1) * HEAD_DIM)
            k_new = _head_rms(kn_ref[rows, ksl], kg)
            kout_ref[sq, keep:window, ksl] = k_new
            k_all = jnp.concatenate([kc_ref[sq, :, ksl], k_new, fill], axis=0)
            v_all = jnp.concatenate([vc_ref[sq, :, ksl], vn_ref[rows, ksl], fill], axis=0)
            heads = [h * Q_PER_KV + g for g in range(Q_PER_KV)]
            q_heads = [_head_rms(q_ref[rows, hq * HEAD_DIM:(hq + 1) * HEAD_DIM], qg) for hq in heads]
            o = _kv_group_attention(q_heads, k_all, v_all, [sink_ref[hq] for hq in heads], valid, t_new)
            for g, hq in enumerate(heads):
                o_ref[rows, hq * HEAD_DIM:(hq + 1) * HEAD_DIM] = _bf16(o[g * t_new:(g + 1) * t_new])


def _attn_sample(a, proj, cache_k, cache_v, layer, sink, q_g, k_g, *, prompt_rows, dec_batch, t_new,
                 d_q, d_kv, window):
    n_kv = d_kv // HEAD_DIM
    depth = cache_k.shape[0]
    seqs = _exact_div(_BF16_ROWS, t_new) if _BF16_ROWS % t_new == 0 and dec_batch % (_BF16_ROWS // t_new) == 0 else 1
    tr = seqs * t_new
    row0 = _exact_div(prompt_rows, tr)
    k_col = _exact_div(d_q, d_kv)
    kern = functools.partial(_attn_sample_kernel, n_kv=n_kv, window=window, t_new=t_new, seqs=seqs)
    kc = cache_k.reshape(depth, dec_batch, window, d_kv)
    vc = cache_v.reshape(depth, dec_batch, window, d_kv)
    buf_spec = pl.BlockSpec((seqs, window, d_kv), lambda i, s: (i, 0, 0))
    grid_spec = pltpu.PrefetchScalarGridSpec(
        num_scalar_prefetch=1,
        grid=(dec_batch // seqs,),
        in_specs=[pl.BlockSpec(memory_space=pl.ANY),
                  pl.BlockSpec((tr, d_q), lambda i, s: (row0 + i, 0)),
                  pl.BlockSpec((tr, d_kv), lambda i, s: (row0 + i, k_col)),
                  pl.BlockSpec((tr, d_kv), lambda i, s: (row0 + i, k_col + 1)),
                  pl.BlockSpec((None, seqs, window, d_kv), lambda i, s: (layer, i, 0, 0)),
                  pl.BlockSpec((None, seqs, window, d_kv), lambda i, s: (layer, i, 0, 0)),
                  pl.BlockSpec((1, HEAD_DIM), lambda i, s: (0, 0)),
                  pl.BlockSpec((1, HEAD_DIM), lambda i, s: (0, 0))],
        out_specs=[pl.BlockSpec((tr, d_q), lambda i, s: (row0 + i, 0)), buf_spec, buf_spec],
    )
    buf_shape = jax.ShapeDtypeStruct((dec_batch, window, d_kv), jnp.float32)
    return pl.pallas_call(
        kern,
        grid_spec=grid_spec,
        out_shape=[jax.ShapeDtypeStruct(a.shape, a.dtype), buf_shape, buf_shape],
        input_output_aliases={1: 0},
        compiler_params=_params(("arbitrary",), tr * (d_q * 6 + d_kv * 8) + seqs * window * d_kv * 16),
        name="attn_sample",
    )(sink, a, proj, proj, proj, kc, vc, q_g.reshape(1, HEAD_DIM), k_g.reshape(1, HEAD_DIM))


def _depthwise_conv(uext_ref, base, n_rows, w_ref, b_ref, conv_ref, conv_base, part_ref):
    d_conv = conv_ref.shape[-1]
    ch_chunk = part_ref.shape[-1]
    for c0 in range(0, d_conv, ch_chunk):
        cs = slice(c0, c0 + ch_chunk)
        for s in range(V7X_SUBLANES):
            span = n_rows + -(-(CONV_LEAD + s) // V7X_SUBLANES) * V7X_SUBLANES
            for t0 in range(0, span, CONV_ROW_CHUNK):
                rc = min(CONV_ROW_CHUNK, span - t0)
                acc = None
                for j in range(s, CONV_WIDTH, V7X_SUBLANES):
                    start = base + t0 + j - s
                    term = uext_ref[start:start + rc, cs] * w_ref[j:j + 1, cs]
                    acc = term if acc is None else acc + term
                part_ref[s, t0:t0 + rc, :] = acc
        for r0 in range(0, n_rows, CONV_ROW_CHUNK):
            rc = min(CONV_ROW_CHUNK, n_rows - r0)
            acc = jnp.broadcast_to(b_ref[:, cs], (rc, ch_chunk))
            for s in range(V7X_SUBLANES):
                off = r0 + CONV_LEAD + s
                acc = acc + part_ref[s, off:off + rc, :]
            conv_ref[conv_base + r0:conv_base + r0 + rc, cs] = acc


def _conv_ch_chunk(d_conv):
    return _pick_tile(d_conv, CONV_CH_CHUNK, V7X_LANES)


def _layer_norm_swish(x, g, b):
    mu = jnp.mean(x, axis=-1, keepdims=True)
    xc = x - mu
    var = jnp.mean(xc * xc, axis=-1, keepdims=True)
    y = xc * lax.rsqrt(var + EPS) * g + b
    return y * jax.nn.sigmoid(y)


def _conv_prompt_kernel(uv_ref, ug_ref, w_ref, b_ref, g_ref, bb_ref, o_ref, st_ref, uext_ref, conv_ref,
                        part_ref, *, tt):
    @pl.when(pl.program_id(1) == 0)
    def _():
        uext_ref[0:CONV_HALO, :] = jnp.zeros((CONV_HALO, uext_ref.shape[1]), jnp.float32)

    uext_ref[CONV_HALO:CONV_HALO + tt, :] = uv_ref[...] * jax.nn.sigmoid(ug_ref[...])
    _depthwise_conv(uext_ref, 0, tt, w_ref, b_ref, conv_ref, 0, part_ref)
    o_ref[...] = _bf16(_layer_norm_swish(conv_ref[...], g_ref[...], bb_ref[...]))
    st_ref[...] = uext_ref[tt + CONV_LEAD:tt + CONV_HALO, :]
    uext_ref[0:CONV_HALO, :] = uext_ref[tt:tt + CONV_HALO, :]


def _conv_prompt(proj, dw_w, dw_b, cn_g, cn_b, *, batch, seq, total_rows, d_q, d_kv, d_conv):
    tt = _pick_tile(seq, 256, 32)
    nt = seq // tt
    uv_col = _exact_div(d_q + 2 * d_kv, d_conv)
    kern = functools.partial(_conv_prompt_kernel, tt=tt)
    return pl.pallas_call(
        kern,
        grid=(batch, nt),
        in_specs=[pl.BlockSpec((tt, d_conv), lambda b, t: (b * nt + t, uv_col)),
                  pl.BlockSpec((tt, d_conv), lambda b, t: (b * nt + t, uv_col + 1)),
                  pl.BlockSpec((CONV_WIDTH, d_conv), lambda b, t: (0, 0)),
                  pl.BlockSpec((1, d_conv), lambda b, t: (0, 0)),
                  pl.BlockSpec((1, d_conv), lambda b, t: (0, 0)),
                  pl.BlockSpec((1, d_conv), lambda b, t: (0, 0))],
        out_specs=[pl.BlockSpec((tt, d_conv), lambda b, t: (b * nt + t, 0)),
                   pl.BlockSpec((None, CONV_WIDTH - 1, d_conv), lambda b, t: (b, 0, 0))],
        out_shape=[jax.ShapeDtypeStruct((total_rows, d_conv), MXU_DTYPE),
                   jax.ShapeDtypeStruct((batch, CONV_WIDTH - 1, d_conv), jnp.float32)],
        scratch_shapes=[pltpu.VMEM((tt + CONV_HALO, d_conv), jnp.float32),
                        pltpu.VMEM((tt, d_conv), jnp.float32),
                        pltpu.VMEM((V7X_SUBLANES, tt + 2 * V7X_SUBLANES, _conv_ch_chunk(d_conv)), jnp.float32)],
        compiler_params=_params(("arbitrary", "arbitrary"), tt * d_conv * 18),
        name="conv_prompt",
    )(proj, proj, dw_w, dw_b.reshape(1, d_conv), cn_g.reshape(1, d_conv), cn_b.reshape(1, d_conv))


def _conv_sample_kernel(cv_in_ref, uv_ref, ug_ref, st_ref, w_ref, b_ref, g_ref, bb_ref, o_ref, nst_ref,
                        uext_ref, conv_ref, part_ref, *, t_new, seqs):
    del cv_in_ref
    ext = CONV_HALO + t_new
    for sq in range(seqs):
        base = sq * ext
        rows = slice(sq * t_new, (sq + 1) * t_new)
        uext_ref[base:base + CONV_LEAD, :] = jnp.zeros((CONV_LEAD, uext_ref.shape[1]), jnp.float32)
        uext_ref[base + CONV_LEAD:base + CONV_HALO, :] = st_ref[sq]
        uext_ref[base + CONV_HALO:base + ext, :] = uv_ref[rows, :] * jax.nn.sigmoid(ug_ref[rows, :])
        nst_ref[sq] = uext_ref[base + ext - (CONV_WIDTH - 1):base + ext, :]
        _depthwise_conv(uext_ref, base, t_new, w_ref, b_ref, conv_ref, sq * t_new, part_ref)
    o_ref[...] = _bf16(_layer_norm_swish(conv_ref[...], g_ref[...], bb_ref[...]))


def _conv_sample(cv, proj, state_conv, layer, dw_w, dw_b, cn_g, cn_b, *, prompt_rows, dec_batch, t_new,
                 d_q, d_kv, d_conv):
    seqs = _exact_div(_BF16_ROWS, t_new) if _BF16_ROWS % t_new == 0 and dec_batch % (_BF16_ROWS // t_new) == 0 else 1
    tr = seqs * t_new
    row0 = _exact_div(prompt_rows, tr)
    uv_col = _exact_div(d_q + 2 * d_kv, d_conv)
    kern = functools.partial(_conv_sample_kernel, t_new=t_new, seqs=seqs)
    return pl.pallas_call(
        kern,
        grid=(dec_batch // seqs,),
        in_specs=[pl.BlockSpec(memory_space=pl.ANY),
                  pl.BlockSpec((tr, d_conv), lambda i: (row0 + i, uv_col)),
                  pl.BlockSpec((tr, d_conv), lambda i: (row0 + i, uv_col + 1)),
                  pl.BlockSpec((None, seqs, CONV_WIDTH - 1, d_conv), lambda i: (layer, i, 0, 0)),
                  pl.BlockSpec((CONV_WIDTH, d_conv), lambda i: (0, 0)),
                  pl.BlockSpec((1, d_conv), lambda i: (0, 0)),
                  pl.BlockSpec((1, d_conv), lambda i: (0, 0)),
                  pl.BlockSpec((1, d_conv), lambda i: (0, 0))],
        out_specs=[pl.BlockSpec((tr, d_conv), lambda i: (row0 + i, 0)),
                   pl.BlockSpec((seqs, CONV_WIDTH - 1, d_conv), lambda i: (i, 0, 0))],
        out_shape=[jax.ShapeDtypeStruct(cv.shape, cv.dtype),
                   jax.ShapeDtypeStruct((dec_batch, CONV_WIDTH - 1, d_conv), jnp.float32)],
        input_output_aliases={0: 0},
        scratch_shapes=[pltpu.VMEM((seqs * (CONV_HALO + t_new), d_conv), jnp.float32),
                        pltpu.VMEM((tr, d_conv), jnp.float32),
                        pltpu.VMEM((V7X_SUBLANES, t_new + 2 * V7X_SUBLANES, _conv_ch_chunk(d_conv)), jnp.float32)],
        compiler_params=_params(("arbitrary",), (tr * 12 + seqs * 64 * 4 + 40 * 4) * d_conv),
        name="conv_sample",
    )(cv, proj, proj, state_conv, dw_w, dw_b.reshape(1, d_conv), cn_g.reshape(1, d_conv), cn_b.reshape(1, d_conv))


def _merge_kernel(a_ref, cv_ref, wao_ref, wco_ref, ga_ref, gb_ref, o_ref):
    pa = jnp.dot(a_ref[...], _bf16(wao_ref[...]), preferred_element_type=jnp.float32)
    pc = jnp.dot(cv_ref[...], _bf16(wco_ref[...]), preferred_element_type=jnp.float32)
    o_ref[...] = _bf16(jax.nn.sigmoid(ga_ref[...]) * pa + jax.nn.sigmoid(gb_ref[...]) * pc)


def _merge(a, cv, proj, w_ao, w_co, layer, *, ga_off):
    rows, d_q = a.shape
    d_conv = cv.shape[1]
    d = w_ao.shape[2]
    tm = _pick_tile(rows, 384, _BF16_ROWS)
    tn = _pick_tile(d, 512, V7X_LANES)
    ga_col = _exact_div(ga_off, tn)
    gb_col = ga_col + d // tn
    blk = tm * (d_q + d_conv) * 2 + (d_q + d_conv) * tn * 6 + tm * tn * 10
    return pl.pallas_call(
        _merge_kernel,
        grid=(d // tn, rows // tm),
        in_specs=[pl.BlockSpec((tm, d_q), lambda j, i: (i, 0)),
                  pl.BlockSpec((tm, d_conv), lambda j, i: (i, 0)),
                  pl.BlockSpec((None, d_q, tn), lambda j, i: (layer, 0, j)),
                  pl.BlockSpec((None, d_conv, tn), lambda j, i: (layer, 0, j)),
                  pl.BlockSpec((tm, tn), lambda j, i: (i, ga_col + j)),
                  pl.BlockSpec((tm, tn), lambda j, i: (i, gb_col + j))],
        out_specs=pl.BlockSpec((tm, tn), lambda j, i: (i, j)),
        out_shape=jax.ShapeDtypeStruct((rows, d), MXU_DTYPE),
        compiler_params=_params(("arbitrary", "arbitrary"), blk),
        name="merge",
    )(a, cv, w_ao, w_co, proj, proj)


def _out_proj_kernel(z_ref, w_ref, x_ref, g_ref, o_ref):
    m = jnp.dot(z_ref[...], _bf16(w_ref[...]), preferred_element_type=jnp.float32)
    for r in range(0, m.shape[0], V7X_SUBLANES):
        gi = r // V7X_SUBLANES
        rs = slice(r, r + V7X_SUBLANES)
        o_ref[rs, :] = x_ref[rs, :] + g_ref[gi:gi + 1, :] * m[rs]


def _out_proj(z, w_out, x, modg, layer, *, gate_col):
    rows, d = x.shape
    tm = _pick_tile(rows, 1408, _BF16_ROWS)
    tn = _pick_tile(d, 512, V7X_LANES)
    gcol = gate_col * (d // tn)
    blk = tm * d * 2 + d * tn * 6 + tm * tn * 9
    return pl.pallas_call(
        _out_proj_kernel,
        grid=(d // tn, rows // tm),
        in_specs=[pl.BlockSpec((tm, d), lambda j, i: (i, 0)),
                  pl.BlockSpec((None, d, tn), lambda j, i: (layer, 0, j)),
                  pl.BlockSpec((tm, tn), lambda j, i: (i, j)),
                  pl.BlockSpec((tm // V7X_SUBLANES, tn), lambda j, i: (i, gcol + j))],
        out_specs=pl.BlockSpec((tm, tn), lambda j, i: (i, j)),
        out_shape=jax.ShapeDtypeStruct((rows, d), jnp.float32),
        compiler_params=_params(("arbitrary", "arbitrary"), blk),
        name="out_proj",
    )(z, w_out, x, modg)


def _first_argmax(x, axis):
    m = jnp.max(x, axis=axis, keepdims=True)
    idx = lax.broadcasted_iota(jnp.int32, x.shape, axis)
    first = jnp.min(jnp.where(x == m, idx, x.shape[axis]), axis=axis, keepdims=True)
    return m, first


def _router_kernel(x_ref, g_ref, sc_ref, sh_ref, rwt_ref, rb_ref, h_ref, eid_ref, wt_ref, rank_ref, cnt_ref,
                   *, n_experts):
    tr = x_ref.shape[0]
    for r in range(0, tr, V7X_SUBLANES):
        h_ref[r:r + V7X_SUBLANES, :] = _modulated_norm_rows(x_ref, g_ref, sc_ref, sh_ref, r, V7X_SUBLANES)
    per_group = n_experts // N_GROUPS
    logits = lax.dot_general(rwt_ref[...], h_ref[...], (((1,), (1,)), ((), ())),
                             precision=lax.Precision.HIGHEST,
                             preferred_element_type=jnp.float32)
    s = jax.nn.sigmoid(logits)
    sel = (s + rb_ref[...]).reshape(N_GROUPS, per_group, tr)
    m1, i1 = _first_argmax(sel, 1)
    e_iota = lax.broadcasted_iota(jnp.int32, sel.shape, 1)
    m2 = jnp.max(jnp.where(e_iota == i1, -jnp.inf, sel), axis=1, keepdims=True)
    _, grp = _first_argmax(m1 + m2, 0)
    g_iota = lax.broadcasted_iota(jnp.int32, sel.shape, 0)
    in_group = jnp.max(jnp.where(g_iota == grp, sel, -jnp.inf), axis=0, keepdims=True)
    _, j1 = _first_argmax(in_group, 1)
    j_iota = lax.broadcasted_iota(jnp.int32, in_group.shape, 1)
    _, j2 = _first_argmax(jnp.where(j_iota == j1, -jnp.inf, in_group), 1)
    e1 = (grp * per_group + j1).reshape(1, tr)
    e2 = (grp * per_group + j2).reshape(1, tr)
    x_iota = lax.broadcasted_iota(jnp.int32, s.shape, 0)
    hit1 = x_iota == e1
    hit2 = x_iota == e2
    w1 = jnp.sum(jnp.where(hit1, s, 0.0), axis=0, keepdims=True)
    w2 = jnp.sum(jnp.where(hit2, s, 0.0), axis=0, keepdims=True)
    tot = w1 + w2
    eid_ref[...] = jnp.concatenate([e1, e2], axis=0)
    wt_ref[...] = jnp.concatenate([w1 / tot, w2 / tot], axis=0)

    @pl.when(pl.program_id(0) == 0)
    def _():
        cnt_ref[...] = jnp.zeros(cnt_ref.shape, jnp.float32)

    one1 = jnp.where(hit1, 1.0, 0.0)
    one2 = jnp.where(hit2, 1.0, 0.0)
    before = (lax.broadcasted_iota(jnp.int32, (tr, tr), 0) <
              lax.broadcasted_iota(jnp.int32, (tr, tr), 1)).astype(MXU_DTYPE)
    pre1 = jnp.dot(_bf16(one1), before, preferred_element_type=jnp.float32)
    pre2 = jnp.dot(_bf16(one2), before, preferred_element_type=jnp.float32)
    tot1 = jnp.sum(one1, axis=1, keepdims=True)
    tot2 = jnp.sum(one2, axis=1, keepdims=True)
    base = cnt_ref[:, 0:1]
    r1 = jnp.sum(one1 * (base + pre1), axis=0, keepdims=True)
    r2 = jnp.sum(one2 * (base + tot1 + pre2), axis=0, keepdims=True)
    rank_ref[...] = jnp.concatenate([r1, r2], axis=0).astype(jnp.int32)
    cnt_ref[...] = cnt_ref[...] + (tot1 + tot2)


def _norm_router(x, g, modg, router_w, router_b, *, shift_col, scale_col):
    rows, d = x.shape
    n_experts = router_w.shape[1]
    tr = _pick_tile(rows, 256, V7X_LANES) if rows % V7X_LANES == 0 else rows
    tg = tr // V7X_SUBLANES
    kern = functools.partial(_router_kernel, n_experts=n_experts)
    return pl.pallas_call(
        kern,
        grid=(rows // tr,),
        in_specs=[pl.BlockSpec((tr, d), lambda i: (i, 0)),
                  pl.BlockSpec((1, d), lambda i: (0, 0)),
                  pl.BlockSpec((tg, d), lambda i: (i, scale_col)),
                  pl.BlockSpec((tg, d), lambda i: (i, shift_col)),
                  pl.BlockSpec((n_experts, d), lambda i: (0, 0)),
                  pl.BlockSpec((n_experts, 1), lambda i: (0, 0))],
        out_specs=[pl.BlockSpec((tr, d), lambda i: (i, 0)),
                   pl.BlockSpec((TOP_K, tr), lambda i: (0, i)),
                   pl.BlockSpec((TOP_K, tr), lambda i: (0, i)),
                   pl.BlockSpec((TOP_K, tr), lambda i: (0, i)),
                   pl.BlockSpec((n_experts, V7X_LANES), lambda i: (0, 0))],
        out_shape=[jax.ShapeDtypeStruct((rows, d), jnp.float32),
                   jax.ShapeDtypeStruct((TOP_K, rows), jnp.int32),
                   jax.ShapeDtypeStruct((TOP_K, rows), jnp.float32),
                   jax.ShapeDtypeStruct((TOP_K, rows), jnp.int32),
                   jax.ShapeDtypeStruct((n_experts, V7X_LANES), jnp.float32)],
        compiler_params=_params(("arbitrary",), tr * d * 16 + n_experts * d * 4),
        name="norm_router",
    )(x, g.reshape(1, d), modg, modg, router_w.T, router_b.reshape(n_experts, 1))


def _dispatch_kernel(eid_ref, rank_ref, cnt_ref, dest_ref, bexp_ref, bcnt_ref, nused_ref, *, block_rows):
    n_experts = cnt_ref.shape[0]
    lanes = bexp_ref.shape[1]
    counts = cnt_ref[...]
    padded = jnp.floor((counts + (block_rows - 0.5)) * (1.0 / block_rows)) * block_rows
    upto = (lax.broadcasted_iota(jnp.int32, (n_experts, n_experts), 1) <=
            lax.broadcasted_iota(jnp.int32, (n_experts, n_experts), 0)).astype(jnp.float32)
    pend = jnp.dot(upto, padded, precision=lax.Precision.HIGHEST, preferred_element_type=jnp.float32)
    poff = pend - padded
    e_iota = lax.broadcasted_iota(jnp.int32, (n_experts, eid_ref.shape[1]), 0)
    for k in range(TOP_K):
        off = jnp.sum(jnp.where(e_iota == eid_ref[k:k + 1, :], poff[:, 0:1], 0.0), axis=0, keepdims=True)
        dest_ref[k:k + 1, :] = off.astype(jnp.int32) + rank_ref[k:k + 1, :]
    n_used = jnp.floor((jnp.max(pend, axis=0, keepdims=True)[:, 0:1] + 0.5) * (1.0 / block_rows))
    b_all = lax.broadcasted_iota(jnp.int32, (1, lanes), 1).astype(jnp.float32)
    start = jnp.minimum(b_all, n_used - 1.0) * block_rows
    ends_before = jnp.where(pend[:, 0:1] <= start, 1.0, 0.0)
    bexp = jnp.minimum(jnp.sum(ends_before, axis=0, keepdims=True), n_experts - 1.0)
    x_iota = lax.broadcasted_iota(jnp.int32, (n_experts, lanes), 0).astype(jnp.float32)
    last_row = jnp.sum(jnp.where(x_iota == bexp, (poff + counts)[:, 0:1], 0.0), axis=0, keepdims=True)
    filled = jnp.clip(last_row - b_all * block_rows, 0.0, float(block_rows))
    bexp_ref[...] = bexp.astype(jnp.int32)
    bcnt_ref[...] = jnp.where(b_all < n_used, filled, 0.0).astype(jnp.int32)
    nused_ref[...] = jnp.broadcast_to(n_used, nused_ref.shape).astype(jnp.int32)


def _dispatch_plan(eid, rank, counts, block_rows):
    top_k, rows = eid.shape
    n_experts = counts.shape[0]
    n_assign = top_k * rows
    n_blocks = -(-(n_assign + n_experts * (block_rows - 1)) // block_rows)
    lanes = -(-n_blocks // V7X_LANES) * V7X_LANES
    tr = _pick_tile(rows, 1536, V7X_LANES) if rows % V7X_LANES == 0 else rows
    kern = functools.partial(_dispatch_kernel, block_rows=block_rows)
    small = lambda n: pl.BlockSpec((1, n), lambda i: (0, 0))
    dest, blk_exp, blk_cnt, n_used = pl.pallas_call(
        kern,
        grid=(rows // tr,),
        in_specs=[pl.BlockSpec((top_k, tr), lambda i: (0, i)),
                  pl.BlockSpec((top_k, tr), lambda i: (0, i)),
                  pl.BlockSpec((n_experts, V7X_LANES), lambda i: (0, 0))],
        out_specs=[pl.BlockSpec((top_k, tr), lambda i: (0, i)), small(lanes), small(lanes), small(V7X_LANES)],
        out_shape=[jax.ShapeDtypeStruct((top_k, rows), jnp.int32),
                   jax.ShapeDtypeStruct((1, lanes), jnp.int32),
                   jax.ShapeDtypeStruct((1, lanes), jnp.int32),
                   jax.ShapeDtypeStruct((1, V7X_LANES), jnp.int32)],
        compiler_params=_params(("arbitrary",), n_experts * tr * 8),
        name="moe_dispatch",
    )(eid, rank, counts)
    tok = jnp.tile(jnp.arange(rows, dtype=jnp.int32), top_k)
    row_tok = jnp.zeros((n_blocks * block_rows,), jnp.int32).at[dest.reshape(n_assign)].set(tok)
    return (row_tok.reshape(n_blocks, 1, block_rows), dest, blk_exp[0, :n_blocks], blk_cnt[0, :n_blocks],
            n_used[0, :1])


def _experts_kernel(exp_ref, cnt_ref, nused_ref, tok0_ref, tok_next_ref, h_ref, wg_hbm, wu_hbm, wd_hbm,
                    o_ref, x_buf, sems, w_buf, w_sems, g_acc, u_acc, h_buf, *, layer):
    b = pl.program_id(0)
    n_used = nused_ref[0]
    used = b < n_used
    slot = b % 2
    p = MOE_PHASE_STEPS
    n_chunks = 3 * p
    d_e = w_buf.shape[1]

    def chunk_copy(blk, c):
        e = exp_ref[blk]
        if c < 2 * p:
            src = (wg_hbm if c % 2 == 0 else wu_hbm).at[layer, e, pl.ds((c // 2) * d_e, d_e), :]
        else:
            src = wd_hbm.at[layer, e, :, pl.ds((c - 2 * p) * d_e, d_e)]
        ring = c % MOE_WEIGHT_SLOTS
        return pltpu.make_async_copy(src, w_buf.at[ring], w_sems.at[ring])

    def row_copy(tok_ref, r, slot_):
        return pltpu.make_async_copy(h_ref.at[pl.ds(tok_ref[0, r], 1)], x_buf.at[slot_, pl.ds(r, 1)],
                                     sems.at[slot_])

    def gather_start(tok_ref, n, slot_):
        def body(grp, c):
            for u in range(DMA_ISSUE_UNROLL):
                row_copy(tok_ref, grp * DMA_ISSUE_UNROLL + u, slot_).start()
            return c

        lax.fori_loop(0, lax.shift_right_logical(n + (DMA_ISSUE_UNROLL - 1), DMA_ISSUE_SHIFT), body, 0)

    def gather_wait(tok_ref, n, slot_):
        def body(grp, c):
            for u in range(DMA_ISSUE_UNROLL):
                row_copy(tok_ref, grp * DMA_ISSUE_UNROLL + u, slot_).wait()
            return c

        lax.fori_loop(0, lax.shift_right_logical(n + (DMA_ISSUE_UNROLL - 1), DMA_ISSUE_SHIFT), body, 0)

    @pl.when(b == 0)
    def _():
        for c in range(MOE_WEIGHT_LOOKAHEAD):
            chunk_copy(0, c).start()
        x_buf[...] = jnp.zeros(x_buf.shape, x_buf.dtype)
        gather_start(tok0_ref, cnt_ref[0], 0)

    @pl.when(used)
    def _():
        gather_wait(tok0_ref, cnt_ref[b], slot)

        @pl.when(b + 1 < n_used)
        def _():
            gather_start(tok_next_ref, cnt_ref[b + 1], 1 - slot)

        for c in range(n_chunks):
            chunk_copy(b, c).wait()
            ahead = c + MOE_WEIGHT_LOOKAHEAD
            if ahead < n_chunks:
                chunk_copy(b, ahead).start()
            else:
                @pl.when(b + 1 < n_used)
                def _(ahead=ahead):
                    chunk_copy(b + 1, ahead - n_chunks).start()

            w = _bf16(w_buf[c % MOE_WEIGHT_SLOTS])
            if c < 2 * p:
                k = c // 2
                x = _bf16(x_buf[slot, :, k * d_e:(k + 1) * d_e])
                part = jnp.dot(x, w, preferred_element_type=jnp.float32)
                acc = g_acc if c % 2 == 0 else u_acc
                if k == 0:
                    acc[...] = part
                else:
                    acc[...] += part
                if c == 2 * p - 1:
                    g = g_acc[...]
                    h_buf[...] = _bf16(g * jax.nn.sigmoid(g) * u_acc[...])
            else:
                n = c - 2 * p
                o_ref[:, n * d_e:(n + 1) * d_e] = jnp.dot(h_buf[...], w, preferred_element_type=jnp.float32)


def _experts(h, row_tok, w_gate, w_up, w_down, layer, blk_exp, blk_cnt, n_used):
    d = h.shape[1]
    d_e = w_gate.shape[3]
    n_blocks, _, block_rows = row_tok.shape
    n_rows = n_blocks * block_rows
    if d_e * MOE_PHASE_STEPS != d or (3 * MOE_PHASE_STEPS) % MOE_WEIGHT_SLOTS:
        raise ValueError("expert weight chunks must be square and fill the ring a whole number of times")
    if block_rows % DMA_ISSUE_UNROLL:
        raise ValueError("expert block rows must be a whole number of DMA issue groups")

    def out_map(b, exp, cnt, nu):
        return (jnp.maximum(jnp.minimum(b, nu[0] - 1), 0), 0)

    def tok_next_map(b, exp, cnt, nu):
        return (jnp.maximum(jnp.minimum(b + 1, nu[0] - 1), 0), 0, 0)

    grid_spec = pltpu.PrefetchScalarGridSpec(
        num_scalar_prefetch=3,
        grid=(n_blocks,),
        in_specs=[pl.BlockSpec((None, 1, block_rows), lambda b, exp, cnt, nu: (0, 0, 0), memory_space=pltpu.SMEM),
                  pl.BlockSpec((None, 1, block_rows), tok_next_map, memory_space=pltpu.SMEM),
                  pl.BlockSpec(memory_space=pl.ANY),
                  pl.BlockSpec(memory_space=pl.ANY),
                  pl.BlockSpec(memory_space=pl.ANY),
                  pl.BlockSpec(memory_space=pl.ANY)],
        out_specs=pl.BlockSpec((block_rows, d), out_map),
        scratch_shapes=[pltpu.VMEM((2, block_rows, d), jnp.float32),
                        pltpu.SemaphoreType.DMA((2,)),
                        pltpu.VMEM((MOE_WEIGHT_SLOTS, d_e, d_e), jnp.float32),
                        pltpu.SemaphoreType.DMA((MOE_WEIGHT_SLOTS,)),
                        pltpu.VMEM((block_rows, d_e), jnp.float32),
                        pltpu.VMEM((block_rows, d_e), jnp.float32),
                        pltpu.VMEM((block_rows, d_e), MXU_DTYPE)],
    )
    blk_bytes = (block_rows * d * 4 * 2 + MOE_WEIGHT_SLOTS * d_e * d_e * 2 + d_e * d_e * 2 + block_rows * d_e * 5)
    return pl.pallas_call(
        functools.partial(_experts_kernel, layer=layer),
        grid_spec=grid_spec,
        out_shape=jax.ShapeDtypeStruct((n_rows, d), jnp.float32),
        compiler_params=_params(("arbitrary",), blk_bytes),
        name="moe_experts",
    )(blk_exp, blk_cnt, n_used, row_tok, row_tok, h, w_gate, w_up, w_down)


def _combine_kernel(dest_ref, y_ref, x_ref, wt_ref, g_ref, *refs, tr, total_rows, prompt_steps):
    out_refs, (ybuf, sems) = refs[:-2], refs[-2:]
    i = pl.program_id(0)
    slot = i % 2

    def copy(step, slot_, k, r):
        row = dest_ref[k * total_rows + step * tr + r]
        return pltpu.make_async_copy(y_ref.at[pl.ds(row, 1)], ybuf.at[slot_, k, pl.ds(r, 1)], sems.at[slot_])

    def gather_start(step, slot_):
        def body(grp, c):
            for u in range(DMA_ISSUE_UNROLL):
                for k in range(TOP_K):
                    copy(step, slot_, k, grp * DMA_ISSUE_UNROLL + u).start()
            return c

        lax.fori_loop(0, tr // DMA_ISSUE_UNROLL, body, 0)

    @pl.when(i == 0)
    def _():
        gather_start(0, 0)

    @pl.when(i + 1 < pl.num_programs(0))
    def _():
        gather_start(i + 1, 1 - slot)

    def wait(grp, c):
        for u in range(DMA_ISSUE_UNROLL):
            for k in range(TOP_K):
                copy(i, slot, k, grp * DMA_ISSUE_UNROLL + u).wait()
        return c

    lax.fori_loop(0, tr // DMA_ISSUE_UNROLL, wait, 0)

    def emit(o_ref):
        for r in range(0, tr, V7X_SUBLANES):
            gi = r // V7X_SUBLANES
            rs = slice(r, r + V7X_SUBLANES)
            y = wt_ref[rs, 0:1] * ybuf[slot, 0, rs, :] + wt_ref[rs, 1:2] * ybuf[slot, 1, rs, :]
            o_ref[rs, :] = x_ref[rs, :] + g_ref[gi:gi + 1, :] * y

    if prompt_steps is None:
        emit(out_refs[0])
    else:
        pl.when(i < prompt_steps)(lambda: emit(out_refs[0]))
        pl.when(i >= prompt_steps)(lambda: emit(out_refs[1]))


def _combine(yb, dest, wts, x, modg, *, gate_col, split_rows=None):
    rows, d = x.shape
    if split_rows is None:
        tr = _pick_tile(rows, 128, V7X_SUBLANES)
        p_steps = None
        out_specs = pl.BlockSpec((tr, d), lambda i, dst: (i, 0))
        out_shape = jax.ShapeDtypeStruct((rows, d), jnp.float32)
    else:
        tr = _pick_tile(math.gcd(split_rows, rows - split_rows), 128, V7X_SUBLANES)
        p_steps = split_rows // tr
        out_specs = [pl.BlockSpec((tr, d), lambda i, dst: (jnp.minimum(i, p_steps - 1), 0)),
                     pl.BlockSpec((tr, d), lambda i, dst: (jnp.maximum(i - p_steps, 0), 0))]
        out_shape = [jax.ShapeDtypeStruct((split_rows, d), jnp.float32),
                     jax.ShapeDtypeStruct((rows - split_rows, d), jnp.float32)]
    kern = functools.partial(_combine_kernel, tr=tr, total_rows=rows, prompt_steps=p_steps)
    grid_spec = pltpu.PrefetchScalarGridSpec(
        num_scalar_prefetch=1,
        grid=(rows // tr,),
        in_specs=[pl.BlockSpec(memory_space=pl.ANY),
                  pl.BlockSpec((tr, d), lambda i, dst: (i, 0)),
                  pl.BlockSpec((tr, TOP_K), lambda i, dst: (i, 0)),
                  pl.BlockSpec((tr // V7X_SUBLANES, d), lambda i, dst: (i, gate_col))],
        out_specs=out_specs,
        scratch_shapes=[pltpu.VMEM((2, TOP_K, tr, d), yb.dtype),
                        pltpu.SemaphoreType.DMA((2,))],
    )
    return pl.pallas_call(
        kern,
        grid_spec=grid_spec,
        out_shape=out_shape,
        compiler_params=_params(("arbitrary",), tr * d * 20),
        name="moe_combine",
    )(dest.reshape(-1), yb, x, wts.T, modg)


def kernel(x_prompt, x_sample, cache_k, cache_v, state_conv, c_prompt, c_sample, router_w, router_b, norm1_g, norm2_g, w_ada, b_ada, w_in, q_norm_g, k_norm_g, attn_sink, w_ao, dw_w, dw_b, cn_g, cn_b, w_co, w_out, moe_w_gate, moe_w_up, moe_w_down):
    batch, seq, d = x_prompt.shape
    dec_batch, t_new, _ = x_sample.shape
    depth = w_in.shape[0]
    window = cache_k.shape[2]
    n_kv = cache_k.shape[3]
    d_kv = n_kv * cache_k.shape[4]
    d_q = w_ao.shape[1]
    d_conv = w_co.shape[1]
    n_experts = router_w.shape[1]
    if q_norm_g.shape[1] != HEAD_DIM or cache_k.shape[4] != HEAD_DIM or dw_w.shape[1] != CONV_WIDTH:
        raise ValueError("unsupported head / conv geometry")
    if d_q != Q_PER_KV * d_kv or n_experts % N_GROUPS:
        raise ValueError("unsupported head / expert grouping")
    if seq % window or t_new % V7X_SUBLANES or t_new > window:
        raise ValueError("unsupported sequence geometry")
    prompt_rows = batch * seq
    sample_rows = dec_batch * t_new
    rows = prompt_rows + sample_rows
    ga_off = d_q + 2 * d_kv + 2 * d_conv

    SH1, SC1, G1, SH2, SC2, G2 = range(6)

    kp, vp, up, ks, vs, us = [], [], [], [], [], []
    for l in range(depth):
        mg = _adaln(c_prompt, c_sample, w_ada, b_ada, l, prompt_groups=seq // V7X_SUBLANES,
                    sample_groups=t_new // V7X_SUBLANES)
        if l == 0:
            h, x = _norm_mod_stack(x_prompt.reshape(prompt_rows, d), x_sample.reshape(sample_rows, d),
                                   norm1_g[l], mg, SH1, SC1)
        else:
            h = _norm_mod(x, norm1_g[l], mg, SH1, SC1)
        proj = _in_proj(h, w_in, l)

        a, k_last, v_last = _attn_prompt(proj, attn_sink[l], q_norm_g[l], k_norm_g[l], batch=batch, seq=seq,
                                         total_rows=rows, d_q=d_q, d_kv=d_kv, window=window)
        a, k_buf, v_buf = _attn_sample(a, proj, cache_k, cache_v, l, attn_sink[l], q_norm_g[l], k_norm_g[l],
                                       prompt_rows=prompt_rows, dec_batch=dec_batch, t_new=t_new,
                                       d_q=d_q, d_kv=d_kv, window=window)
        cv, u_last = _conv_prompt(proj, dw_w[l], dw_b[l], cn_g[l], cn_b[l], batch=batch, seq=seq,
                                  total_rows=rows, d_q=d_q, d_kv=d_kv, d_conv=d_conv)
        cv, u_new = _conv_sample(cv, proj, state_conv, l, dw_w[l], dw_b[l], cn_g[l], cn_b[l],
                                 prompt_rows=prompt_rows, dec_batch=dec_batch, t_new=t_new,
                                 d_q=d_q, d_kv=d_kv, d_conv=d_conv)

        z = _merge(a, cv, proj, w_ao, w_co, l, ga_off=ga_off)
        x = _out_proj(z, w_out, x, mg, l, gate_col=G1)

        h2, eid, wts, rank, counts = _norm_router(x, norm2_g[l], mg, router_w, router_b,
                                                  shift_col=SH2, scale_col=SC2)
        row_tok, dest, blk_exp, blk_cnt, n_used = _dispatch_plan(eid, rank, counts, MOE_BLOCK_ROWS)
        yb = _experts(h2, row_tok, moe_w_gate, moe_w_up, moe_w_down, l, blk_exp, blk_cnt, n_used)
        if l + 1 < depth:
            x = _combine(yb, dest, wts, x, mg, gate_col=G2)
        else:
            y_prompt, y_sample = _combine(yb, dest, wts, x, mg, gate_col=G2, split_rows=prompt_rows)

        kp.append(k_last)
        vp.append(v_last)
        up.append(u_last)
        ks.append(k_buf)
        vs.append(v_buf)
        us.append(u_new)

    def heads(bufs, n_seq):
        return jnp.stack(bufs).reshape(depth, n_seq, window, n_kv, HEAD_DIM)

    y_prompt = y_prompt.reshape(batch, seq, d)
    y_sample = y_sample.reshape(dec_batch, t_new, d)
    return (y_prompt, y_sample, heads(kp, batch), heads(vp, batch), jnp.stack(up),
            heads(ks, dec_batch), heads(vs, dec_batch), jnp.stack(us))
```

```python
import functools
import math

import jax
import jax.numpy as jnp
from jax import lax
from jax.experimental import pallas as pl
from jax.experimental.pallas import tpu as pltpu

EPS = 1e-6
HEAD_DIM = 128
Q_PER_KV = 4
CONV_WIDTH = 31
N_GROUPS = 8
TOP_K = 2

V7X_SUBLANES = 8
V7X_LANES = 128
V7X_VMEM_BYTES = 64 * 1024 * 1024
_MIB = 1024 * 1024
_BF16_ROWS = 2 * V7X_SUBLANES
MXU_DTYPE = jnp.bfloat16

CONV_HALO = 32
CONV_LEAD = CONV_HALO - (CONV_WIDTH - 1)
CONV_ROW_CHUNK = 64
CONV_CH_CHUNK = 256
MOE_BLOCK_ROWS = 448
MOE_PHASE_STEPS = 4
MOE_WEIGHT_SLOTS = 4
MOE_WEIGHT_LOOKAHEAD = MOE_WEIGHT_SLOTS - 1
DMA_ISSUE_SHIFT = 3
DMA_ISSUE_UNROLL = 1 << DMA_ISSUE_SHIFT


def _bf16(x):
    return x.astype(MXU_DTYPE)


def _pack_rows(x, chunk):
    half = chunk // 2
    out = []
    for c0 in range(0, x.shape[1], chunk):
        lo = lax.bitcast_convert_type(x[:, c0:c0 + half].astype(jnp.bfloat16).astype(jnp.float32), jnp.uint32)
        hi = lax.bitcast_convert_type(x[:, c0 + half:c0 + chunk].astype(jnp.bfloat16).astype(jnp.float32),
                                      jnp.uint32)
        out.append(lax.shift_right_logical(lo, jnp.uint32(16)) | hi)
    return out[0] if len(out) == 1 else jnp.concatenate(out, axis=1)


def _unpack_rows(p, chunk):
    half = chunk // 2
    out = []
    for c0 in range(0, p.shape[1], half):
        w = p[:, c0:c0 + half]
        out.append(lax.bitcast_convert_type(lax.shift_left(w, jnp.uint32(16)), jnp.float32))
        out.append(lax.bitcast_convert_type(w & jnp.uint32(0xFFFF0000), jnp.float32))
    return jnp.concatenate(out, axis=1)


def _pick_tile(n, target, align):
    best = None
    for t in range(align, min(n, target) + 1, align):
        if n % t == 0:
            best = t
    if best is None:
        raise ValueError(f"no tile for n={n} target={target} align={align}")
    return best


def _exact_div(a, b):
    q, r = divmod(a, b)
    if r:
        raise ValueError(f"{a} is not a multiple of {b}")
    return q


def _params(semantics, block_bytes):
    need = 2 * block_bytes + 8 * _MIB
    limit = int(min(max(need, 16 * _MIB), V7X_VMEM_BYTES - 8 * _MIB))
    return pltpu.CompilerParams(dimension_semantics=semantics, vmem_limit_bytes=limit)


def _adaln_kernel(c_ref, w_ref, b_ref, o_ref, *, batch, sample_row0, dec_batch, prompt_groups, sample_groups):
    c = c_ref[...]
    a = _bf16(c * jax.nn.sigmoid(c))
    mod = jnp.dot(a, _bf16(w_ref[...]), preferred_element_type=jnp.float32) + b_ref[...]
    tn = mod.shape[1]
    for i in range(batch):
        o_ref[i * prompt_groups:(i + 1) * prompt_groups, :] = jnp.broadcast_to(mod[i:i + 1, :], (prompt_groups, tn))
    base = batch * prompt_groups
    if sample_groups == 1:
        o_ref[base:base + dec_batch, :] = mod[sample_row0:sample_row0 + dec_batch, :]
    else:
        for j in range(dec_batch):
            o_ref[base + j * sample_groups:base + (j + 1) * sample_groups, :] = jnp.broadcast_to(
                mod[sample_row0 + j:sample_row0 + j + 1, :], (sample_groups, tn))


def _adaln(c_prompt, c_sample, w_ada, b_ada, layer, *, prompt_groups, sample_groups):
    depth, d, n = w_ada.shape
    batch, dec_batch = c_prompt.shape[0], c_sample.shape[0]
    sample_row0 = -(-batch // V7X_SUBLANES) * V7X_SUBLANES
    c_all = jnp.concatenate([c_prompt, jnp.zeros((sample_row0 - batch, d), jnp.float32), c_sample,
                             jnp.zeros((-dec_batch % V7X_SUBLANES, d), jnp.float32)], axis=0)
    rows = c_all.shape[0]
    groups = batch * prompt_groups + dec_batch * sample_groups
    tn = _pick_tile(n, 1024, V7X_LANES)
    kern = functools.partial(_adaln_kernel, batch=batch, sample_row0=sample_row0, dec_batch=dec_batch,
                             prompt_groups=prompt_groups, sample_groups=sample_groups)
    return pl.pallas_call(
        kern,
        grid=(n // tn,),
        in_specs=[pl.BlockSpec((rows, d), lambda j: (0, 0)),
                  pl.BlockSpec((None, d, tn), lambda j: (layer, 0, j)),
                  pl.BlockSpec((None, 1, tn), lambda j: (layer, 0, j))],
        out_specs=pl.BlockSpec((groups, tn), lambda j: (0, j)),
        out_shape=jax.ShapeDtypeStruct((groups, n), jnp.float32),
        compiler_params=_params(("arbitrary",), d * tn * 6 + rows * d * 4 + groups * tn * 4),
        name="adaln",
    )(c_all, w_ada, b_ada.reshape(depth, 1, n))


def _modulated_norm_rows(x_ref, g_ref, sc_ref, sh_ref, row0, n_rows):
    g = g_ref[...]
    parts = []
    for r in range(row0, row0 + n_rows, V7X_SUBLANES):
        x = x_ref[r:r + V7X_SUBLANES, :]
        gi = r // V7X_SUBLANES
        y = x * lax.rsqrt(jnp.mean(x * x, axis=-1, keepdims=True) + EPS) * g
        parts.append(y * (1.0 + sc_ref[gi:gi + 1, :]) + sh_ref[gi:gi + 1, :])
    return parts[0] if len(parts) == 1 else jnp.concatenate(parts, axis=0)


def _norm_mod_kernel(x_ref, g_ref, sc_ref, sh_ref, o_ref):
    for r in range(0, o_ref.shape[0], _BF16_ROWS):
        o_ref[r:r + _BF16_ROWS, :] = _bf16(_modulated_norm_rows(x_ref, g_ref, sc_ref, sh_ref, r, _BF16_ROWS))


def _norm_mod(x, g, modg, shift_col, scale_col):
    rows, d = x.shape
    tr = _pick_tile(rows, 256, _BF16_ROWS)
    tg = tr // V7X_SUBLANES
    return pl.pallas_call(
        _norm_mod_kernel,
        grid=(rows // tr,),
        in_specs=[pl.BlockSpec((tr, d), lambda i: (i, 0)),
                  pl.BlockSpec((1, d), lambda i: (0, 0)),
                  pl.BlockSpec((tg, d), lambda i: (i, scale_col)),
                  pl.BlockSpec((tg, d), lambda i: (i, shift_col))],
        out_specs=pl.BlockSpec((tr, d), lambda i: (i, 0)),
        out_shape=jax.ShapeDtypeStruct((rows, d), MXU_DTYPE),
        compiler_params=_params(("arbitrary",), tr * d * 8),
        name="norm_mod",
    )(x, g.reshape(1, d), modg, modg)


def _norm_mod_stack_kernel(xp_ref, xs_ref, g_ref, sc_ref, sh_ref, o_ref, x_ref, *, prompt_steps):
    def emit(src_ref):
        x_ref[...] = src_ref[...]
        _norm_mod_kernel(src_ref, g_ref, sc_ref, sh_ref, o_ref)

    is_prompt = pl.program_id(0) < prompt_steps
    pl.when(is_prompt)(lambda: emit(xp_ref))
    pl.when(jnp.logical_not(is_prompt))(lambda: emit(xs_ref))


def _norm_mod_stack(x_prompt, x_sample, g, modg, shift_col, scale_col):
    p_rows, d = x_prompt.shape
    s_rows = x_sample.shape[0]
    rows = p_rows + s_rows
    tr = _pick_tile(math.gcd(p_rows, s_rows), 256, _BF16_ROWS)
    tg = tr // V7X_SUBLANES
    p_steps = p_rows // tr
    kern = functools.partial(_norm_mod_stack_kernel, prompt_steps=p_steps)
    return pl.pallas_call(
        kern,
        grid=(rows // tr,),
        in_specs=[pl.BlockSpec((tr, d), lambda i: (jnp.minimum(i, p_steps - 1), 0)),
                  pl.BlockSpec((tr, d), lambda i: (jnp.maximum(i - p_steps, 0), 0)),
                  pl.BlockSpec((1, d), lambda i: (0, 0)),
                  pl.BlockSpec((tg, d), lambda i: (i, scale_col)),
                  pl.BlockSpec((tg, d), lambda i: (i, shift_col))],
        out_specs=[pl.BlockSpec((tr, d), lambda i: (i, 0)),
                   pl.BlockSpec((tr, d), lambda i: (i, 0))],
        out_shape=[jax.ShapeDtypeStruct((rows, d), MXU_DTYPE),
                   jax.ShapeDtypeStruct((rows, d), jnp.float32)],
        compiler_params=_params(("arbitrary",), tr * d * 16),
        name="norm_mod_stack",
    )(x_prompt, x_sample, g.reshape(1, d), modg, modg)


def _matmul_kernel(x_ref, w_ref, o_ref):
    o_ref[...] = jnp.dot(x_ref[...], _bf16(w_ref[...]), preferred_element_type=jnp.float32)


def _in_proj(h, w_in, layer):
    rows, k = h.shape
    n = w_in.shape[2]
    tm = _pick_tile(rows, 1408, _BF16_ROWS)
    tn = _pick_tile(n, 512, V7X_LANES)
    return pl.pallas_call(
        _matmul_kernel,
        grid=(n // tn, rows // tm),
        in_specs=[pl.BlockSpec((tm, k), lambda j, i: (i, 0)),
                  pl.BlockSpec((None, k, tn), lambda j, i: (layer, 0, j))],
        out_specs=pl.BlockSpec((tm, tn), lambda j, i: (i, j)),
        out_shape=jax.ShapeDtypeStruct((rows, n), jnp.float32),
        compiler_params=_params(("arbitrary", "arbitrary"), tm * k * 2 + k * tn * 6 + tm * tn * 4),
        name="in_proj",
    )(h, w_in)


def _head_rms(x, g):
    return x * lax.rsqrt(jnp.mean(x * x, axis=-1, keepdims=True) + EPS) * g


def _window_mask(n_q, window, older_present):
    n_rows = Q_PER_KV * n_q
    row = lax.broadcasted_iota(jnp.int32, (n_rows, 2 * window), 0) % n_q
    col = lax.broadcasted_iota(jnp.int32, (n_rows, 2 * window), 1)
    return (col >= row) & (col <= row + window) & ((col >= window) | older_present)


def _kv_group_attention(q_heads, k_all, v_all, sink_vals, valid, n_q):
    q_all = jnp.concatenate(q_heads, axis=0)
    sink_col = jnp.concatenate([jnp.full((n_q, 1), sv, jnp.float32) for sv in sink_vals], axis=0)
    s = lax.dot_general(_bf16(q_all), _bf16(k_all), (((1,), (1,)), ((), ())),
                        preferred_element_type=jnp.float32) * (1.0 / math.sqrt(HEAD_DIM))
    s = jnp.where(valid, s, -jnp.inf)
    m = jnp.maximum(jnp.max(s, axis=-1, keepdims=True), sink_col)
    p = jnp.exp(s - m)
    denom = jnp.sum(p, axis=-1, keepdims=True) + jnp.exp(sink_col - m)
    return jnp.dot(_bf16(p), _bf16(v_all), preferred_element_type=jnp.float32) / denom


def _attn_prompt_kernel(sink_ref, q_ref, kp_ref, kc_ref, vp_ref, vc_ref, qg_ref, kg_ref, o_ref, kn_ref,
                        vn_ref, *, n_kv, window):
    qg = qg_ref[...]
    kg = kg_ref[...]
    valid = _window_mask(window, window, pl.program_id(1) > 0)
    vn_ref[...] = vc_ref[...]
    for h in range(n_kv):
        ksl = slice(h * HEAD_DIM, (h + 1) * HEAD_DIM)
        k_cur = _head_rms(kc_ref[:, ksl], kg)
        kn_ref[:, ksl] = k_cur
        k_all = jnp.concatenate([_head_rms(kp_ref[:, ksl], kg), k_cur], axis=0)
        v_all = jnp.concatenate([vp_ref[:, ksl], vc_ref[:, ksl]], axis=0)
        heads = [h * Q_PER_KV + g for g in range(Q_PER_KV)]
        q_heads = [_head_rms(q_ref[:, hq * HEAD_DIM:(hq + 1) * HEAD_DIM], qg) for hq in heads]
        o = _kv_group_attention(q_heads, k_all, v_all, [sink_ref[hq] for hq in heads], valid, window)
        for g, hq in enumerate(heads):
            o_ref[:, hq * HEAD_DIM:(hq + 1) * HEAD_DIM] = _bf16(o[g * window:(g + 1) * window])


def _attn_prompt(proj, sink, q_g, k_g, *, batch, seq, total_rows, d_q, d_kv, window):
    n_kv = d_kv // HEAD_DIM
    nb = _exact_div(seq, window)
    k_col = _exact_div(d_q, d_kv)
    v_col = k_col + 1
    kern = functools.partial(_attn_prompt_kernel, n_kv=n_kv, window=window)

    def cur(b, i, s):
        return b * nb + i

    def prev(b, i, s):
        return b * nb + jnp.maximum(i - 1, 0)

    grid_spec = pltpu.PrefetchScalarGridSpec(
        num_scalar_prefetch=1,
        grid=(batch, nb),
        in_specs=[pl.BlockSpec((window, d_q), lambda b, i, s: (cur(b, i, s), 0)),
                  pl.BlockSpec((window, d_kv), lambda b, i, s: (prev(b, i, s), k_col)),
                  pl.BlockSpec((window, d_kv), lambda b, i, s: (cur(b, i, s), k_col)),
                  pl.BlockSpec((window, d_kv), lambda b, i, s: (prev(b, i, s), v_col)),
                  pl.BlockSpec((window, d_kv), lambda b, i, s: (cur(b, i, s), v_col)),
                  pl.BlockSpec((1, HEAD_DIM), lambda b, i, s: (0, 0)),
                  pl.BlockSpec((1, HEAD_DIM), lambda b, i, s: (0, 0))],
        out_specs=[pl.BlockSpec((window, d_q), lambda b, i, s: (cur(b, i, s), 0)),
                   pl.BlockSpec((None, window, d_kv), lambda b, i, s: (b, 0, 0)),
                   pl.BlockSpec((None, window, d_kv), lambda b, i, s: (b, 0, 0))],
    )
    return pl.pallas_call(
        kern,
        grid_spec=grid_spec,
        out_shape=[jax.ShapeDtypeStruct((total_rows, d_q), MXU_DTYPE),
                   jax.ShapeDtypeStruct((batch, window, d_kv), jnp.float32),
                   jax.ShapeDtypeStruct((batch, window, d_kv), jnp.float32)],
        compiler_params=_params(("arbitrary", "arbitrary"), window * (d_q * 6 + d_kv * 28)),
        name="attn_prompt",
    )(sink, proj, proj, proj, proj, proj, q_g.reshape(1, HEAD_DIM), k_g.reshape(1, HEAD_DIM))


def _attn_sample_kernel(sink_ref, a_in_ref, q_ref, kn_ref, vn_ref, kc_ref, vc_ref, qg_ref, kg_ref,
                        o_ref, kout_ref, vout_ref, *, n_kv, window, t_new, seqs):
    del a_in_ref
    qg = qg_ref[...]
    kg = kg_ref[...]
    valid = _window_mask(t_new, window, True)
    fill = jnp.zeros((window - t_new, HEAD_DIM), jnp.float32)
    keep = window - t_new
    for sq in range(seqs):
        rows = slice(sq * t_new, (sq + 1) * t_new)
        kout_ref[sq, 0:keep, :] = kc_ref[sq, t_new:window, :]
        vout_ref[sq, 0:keep, :] = vc_ref[sq, t_new:window, :]
        vout_ref[sq, keep:window, :] = vn_ref[rows, :]
        for h in range(n_kv):
            ksl = slice(h * HEAD_DIM, (h + 1) * HEAD_DIM)
            k_new = _head_rms(kn_ref[rows, ksl], kg)
            kout_ref[sq, keep:window, ksl] = k_new
            k_all = jnp.concatenate([kc_ref[sq, :, ksl], k_new, fill], axis=0)
            v_all = jnp.concatenate([vc_ref[sq, :, ksl], vn_ref[rows, ksl], fill], axis=0)
            heads = [h * Q_PER_KV + g for g in range(Q_PER_KV)]
            q_heads = [_head_rms(q_ref[rows, hq * HEAD_DIM:(hq + 1) * HEAD_DIM], qg) for hq in heads]
            o = _kv_group_attention(q_heads, k_all, v_all, [sink_ref[hq] for hq in heads], valid, t_new)
            for g, hq in enumerate(heads):
                o_ref[rows, hq * HEAD_DIM:(hq + 1) * HEAD_DIM] = _bf16(o[g * t_new:(g + 1) * t_new])


def _attn_sample(a, proj, cache_k, cache_v, layer, sink, q_g, k_g, *, prompt_rows, dec_batch, t_new,
                 d_q, d_kv, window):
    n_kv = d_kv // HEAD_DIM
    depth = cache_k.shape[0]
    seqs = _exact_div(_BF16_ROWS, t_new) if _BF16_ROWS % t_new == 0 and dec_batch % (_BF16_ROWS // t_new) == 0 else 1
    tr = seqs * t_new
    row0 = _exact_div(prompt_rows, tr)
    k_col = _exact_div(d_q, d_kv)
    kern = functools.partial(_attn_sample_kernel, n_kv=n_kv, window=window, t_new=t_new, seqs=seqs)
    kc = cache_k.reshape(depth, dec_batch, window, d_kv)
    vc = cache_v.reshape(depth, dec_batch, window, d_kv)
    buf_spec = pl.BlockSpec((seqs, window, d_kv), lambda i, s: (i, 0, 0))
    grid_spec = pltpu.PrefetchScalarGridSpec(
        num_scalar_prefetch=1,
        grid=(dec_batch // seqs,),
        in_specs=[pl.BlockSpec(memory_space=pl.ANY),
                  pl.BlockSpec((tr, d_q), lambda i, s: (row0 + i, 0)),
                  pl.BlockSpec((tr, d_kv), lambda i, s: (row0 + i, k_col)),
                  pl.BlockSpec((tr, d_kv), lambda i, s: (row0 + i, k_col + 1)),
                  pl.BlockSpec((None, seqs, window, d_kv), lambda i, s: (layer, i, 0, 0)),
                  pl.BlockSpec((None, seqs, window, d_kv), lambda i, s: (layer, i, 0, 0)),
                  pl.BlockSpec((1, HEAD_DIM), lambda i, s: (0, 0)),
                  pl.BlockSpec((1, HEAD_DIM), lambda i, s: (0, 0))],
        out_specs=[pl.BlockSpec((tr, d_q), lambda i, s: (row0 + i, 0)), buf_spec, buf_spec],
    )
    buf_shape = jax.ShapeDtypeStruct((dec_batch, window, d_kv), jnp.float32)
    return pl.pallas_call(
        kern,
        grid_spec=grid_spec,
        out_shape=[jax.ShapeDtypeStruct(a.shape, a.dtype), buf_shape, buf_shape],
        input_output_aliases={1: 0},
        compiler_params=_params(("arbitrary",), tr * (d_q * 6 + d_kv * 8) + seqs * window * d_kv * 16),
        name="attn_sample",
    )(sink, a, proj, proj, proj, kc, vc, q_g.reshape(1, HEAD_DIM), k_g.reshape(1, HEAD_DIM))


def _depthwise_conv(uext_ref, base, n_rows, w_ref, b_ref, conv_ref, conv_base, part_ref):
    d_conv = conv_ref.shape[-1]
    ch_chunk = part_ref.shape[-1]
    for c0 in range(0, d_conv, ch_chunk):
        cs = slice(c0, c0 + ch_chunk)
        for s in range(V7X_SUBLANES):
            span = n_rows + -(-(CONV_LEAD + s) // V7X_SUBLANES) * V7X_SUBLANES
            for t0 in range(0, span, CONV_ROW_CHUNK):
                rc = min(CONV_ROW_CHUNK, span - t0)
                acc = None
                for j in range(s, CONV_WIDTH, V7X_SUBLANES):
                    start = base + t0 + j - s
                    term = uext_ref[start:start + rc, cs] * w_ref[j:j + 1, cs]
                    acc = term if acc is None else acc + term
                part_ref[s, t0:t0 + rc, :] = acc
        for r0 in range(0, n_rows, CONV_ROW_CHUNK):
            rc = min(CONV_ROW_CHUNK, n_rows - r0)
            acc = jnp.broadcast_to(b_ref[:, cs], (rc, ch_chunk))
            for s in range(V7X_SUBLANES):
                off = r0 + CONV_LEAD + s
                acc = acc + part_ref[s, off:off + rc, :]
            conv_ref[conv_base + r0:conv_base + r0 + rc, cs] = acc


def _conv_ch_chunk(d_conv):
    return _pick_tile(d_conv, CONV_CH_CHUNK, V7X_LANES)


def _layer_norm_swish(x, g, b):
    mu = jnp.mean(x, axis=-1, keepdims=True)
    xc = x - mu
    var = jnp.mean(xc * xc, axis=-1, keepdims=True)
    y = xc * lax.rsqrt(var + EPS) * g + b
    return y * jax.nn.sigmoid(y)


def _conv_prompt_kernel(uv_ref, ug_ref, w_ref, b_ref, g_ref, bb_ref, o_ref, st_ref, uext_ref, conv_ref,
                        part_ref, *, tt):
    @pl.when(pl.program_id(1) == 0)
    def _():
        uext_ref[0:CONV_HALO, :] = jnp.zeros((CONV_HALO, uext_ref.shape[1]), jnp.float32)

    uext_ref[CONV_HALO:CONV_HALO + tt, :] = uv_ref[...] * jax.nn.sigmoid(ug_ref[...])
    _depthwise_conv(uext_ref, 0, tt, w_ref, b_ref, conv_ref, 0, part_ref)
    o_ref[...] = _bf16(_layer_norm_swish(conv_ref[...], g_ref[...], bb_ref[...]))
    st_ref[...] = uext_ref[tt + CONV_LEAD:tt + CONV_HALO, :]
    uext_ref[0:CONV_HALO, :] = uext_ref[tt:tt + CONV_HALO, :]


def _conv_prompt(proj, dw_w, dw_b, cn_g, cn_b, *, batch, seq, total_rows, d_q, d_kv, d_conv):
    tt = _pick_tile(seq, 256, 32)
    nt = seq // tt
    uv_col = _exact_div(d_q + 2 * d_kv, d_conv)
    kern = functools.partial(_conv_prompt_kernel, tt=tt)
    return pl.pallas_call(
        kern,
        grid=(batch, nt),
        in_specs=[pl.BlockSpec((tt, d_conv), lambda b, t: (b * nt + t, uv_col)),
                  pl.BlockSpec((tt, d_conv), lambda b, t: (b * nt + t, uv_col + 1)),
                  pl.BlockSpec((CONV_WIDTH, d_conv), lambda b, t: (0, 0)),
                  pl.BlockSpec((1, d_conv), lambda b, t: (0, 0)),
                  pl.BlockSpec((1, d_conv), lambda b, t: (0, 0)),
                  pl.BlockSpec((1, d_conv), lambda b, t: (0, 0))],
        out_specs=[pl.BlockSpec((tt, d_conv), lambda b, t: (b * nt + t, 0)),
                   pl.BlockSpec((None, CONV_WIDTH - 1, d_conv), lambda b, t: (b, 0, 0))],
        out_shape=[jax.ShapeDtypeStruct((total_rows, d_conv), MXU_DTYPE),
                   jax.ShapeDtypeStruct((batch, CONV_WIDTH - 1, d_conv), jnp.float32)],
        scratch_shapes=[pltpu.VMEM((tt + CONV_HALO, d_conv), jnp.float32),
                        pltpu.VMEM((tt, d_conv), jnp.float32),
                        pltpu.VMEM((V7X_SUBLANES, tt + 2 * V7X_SUBLANES, _conv_ch_chunk(d_conv)), jnp.float32)],
        compiler_params=_params(("arbitrary", "arbitrary"), tt * d_conv * 18),
        name="conv_prompt",
    )(proj, proj, dw_w, dw_b.reshape(1, d_conv), cn_g.reshape(1, d_conv), cn_b.reshape(1, d_conv))


def _conv_sample_kernel(cv_in_ref, uv_ref, ug_ref, st_ref, w_ref, b_ref, g_ref, bb_ref, o_ref, nst_ref,
                        uext_ref, conv_ref, part_ref, *, t_new, seqs):
    del cv_in_ref
    ext = CONV_HALO + t_new
    for sq in range(seqs):
        base = sq * ext
        rows = slice(sq * t_new, (sq + 1) * t_new)
        uext_ref[base:base + CONV_LEAD, :] = jnp.zeros((CONV_LEAD, uext_ref.shape[1]), jnp.float32)
        uext_ref[base + CONV_LEAD:base + CONV_HALO, :] = st_ref[sq]
        uext_ref[base + CONV_HALO:base + ext, :] = uv_ref[rows, :] * jax.nn.sigmoid(ug_ref[rows, :])
        nst_ref[sq] = uext_ref[base + ext - (CONV_WIDTH - 1):base + ext, :]
        _depthwise_conv(uext_ref, base, t_new, w_ref, b_ref, conv_ref, sq * t_new, part_ref)
    o_ref[...] = _bf16(_layer_norm_swish(conv_ref[...], g_ref[...], bb_ref[...]))


def _conv_sample(cv, proj, state_conv, layer, dw_w, dw_b, cn_g, cn_b, *, prompt_rows, dec_batch, t_new,
                 d_q, d_kv, d_conv):
    seqs = _exact_div(_BF16_ROWS, t_new) if _BF16_ROWS % t_new == 0 and dec_batch % (_BF16_ROWS // t_new) == 0 else 1
    tr = seqs * t_new
    row0 = _exact_div(prompt_rows, tr)
    uv_col = _exact_div(d_q + 2 * d_kv, d_conv)
    kern = functools.partial(_conv_sample_kernel, t_new=t_new, seqs=seqs)
    return pl.pallas_call(
        kern,
        grid=(dec_batch // seqs,),
        in_specs=[pl.BlockSpec(memory_space=pl.ANY),
                  pl.BlockSpec((tr, d_conv), lambda i: (row0 + i, uv_col)),
                  pl.BlockSpec((tr, d_conv), lambda i: (row0 + i, uv_col + 1)),
                  pl.BlockSpec((None, seqs, CONV_WIDTH - 1, d_conv), lambda i: (layer, i, 0, 0)),
                  pl.BlockSpec((CONV_WIDTH, d_conv), lambda i: (0, 0)),
                  pl.BlockSpec((1, d_conv), lambda i: (0, 0)),
                  pl.BlockSpec((1, d_conv), lambda i: (0, 0)),
                  pl.BlockSpec((1, d_conv), lambda i: (0, 0))],
        out_specs=[pl.BlockSpec((tr, d_conv), lambda i: (row0 + i, 0)),
                   pl.BlockSpec((seqs, CONV_WIDTH - 1, d_conv), lambda i: (i, 0, 0))],
        out_shape=[jax.ShapeDtypeStruct(cv.shape, cv.dtype),
                   jax.ShapeDtypeStruct((dec_batch, CONV_WIDTH - 1, d_conv), jnp.float32)],
        input_output_aliases={0: 0},
        scratch_shapes=[pltpu.VMEM((seqs * (CONV_HALO + t_new), d_conv), jnp.float32),
                        pltpu.VMEM((tr, d_conv), jnp.float32),
                        pltpu.VMEM((V7X_SUBLANES, t_new + 2 * V7X_SUBLANES, _conv_ch_chunk(d_conv)), jnp.float32)],
        compiler_params=_params(("arbitrary",), (tr * 12 + seqs * 64 * 4 + 40 * 4) * d_conv),
        name="conv_sample",
    )(cv, proj, proj, state_conv, dw_w, dw_b.reshape(1, d_conv), cn_g.reshape(1, d_conv), cn_b.reshape(1, d_conv))


def _merge_kernel(a_ref, cv_ref, wao_ref, wco_ref, ga_ref, gb_ref, o_ref):
    pa = jnp.dot(a_ref[...], _bf16(wao_ref[...]), preferred_element_type=jnp.float32)
    pc = jnp.dot(cv_ref[...], _bf16(wco_ref[...]), preferred_element_type=jnp.float32)
    o_ref[...] = _bf16(jax.nn.sigmoid(ga_ref[...]) * pa + jax.nn.sigmoid(gb_ref[...]) * pc)


def _merge(a, cv, proj, w_ao, w_co, layer, *, ga_off):
    rows, d_q = a.shape
    d_conv = cv.shape[1]
    d = w_ao.shape[2]
    tm = _pick_tile(rows, 384, _BF16_ROWS)
    tn = _pick_tile(d, 512, V7X_LANES)
    ga_col = _exact_div(ga_off, tn)
    gb_col = ga_col + d // tn
    blk = tm * (d_q + d_conv) * 2 + (d_q + d_conv) * tn * 6 + tm * tn * 10
    return pl.pallas_call(
        _merge_kernel,
        grid=(d // tn, rows // tm),
        in_specs=[pl.BlockSpec((tm, d_q), lambda j, i: (i, 0)),
                  pl.BlockSpec((tm, d_conv), lambda j, i: (i, 0)),
                  pl.BlockSpec((None, d_q, tn), lambda j, i: (layer, 0, j)),
                  pl.BlockSpec((None, d_conv, tn), lambda j, i: (layer, 0, j)),
                  pl.BlockSpec((tm, tn), lambda j, i: (i, ga_col + j)),
                  pl.BlockSpec((tm, tn), lambda j, i: (i, gb_col + j))],
        out_specs=pl.BlockSpec((tm, tn), lambda j, i: (i, j)),
        out_shape=jax.ShapeDtypeStruct((rows, d), MXU_DTYPE),
        compiler_params=_params(("arbitrary", "arbitrary"), blk),
        name="merge",
    )(a, cv, w_ao, w_co, proj, proj)


def _out_proj_kernel(z_ref, w_ref, x_ref, g_ref, o_ref):
    m = jnp.dot(z_ref[...], _bf16(w_ref[...]), preferred_element_type=jnp.float32)
    for r in range(0, m.shape[0], V7X_SUBLANES):
        gi = r // V7X_SUBLANES
        rs = slice(r, r + V7X_SUBLANES)
        o_ref[rs, :] = x_ref[rs, :] + g_ref[gi:gi + 1, :] * m[rs]


def _out_proj(z, w_out, x, modg, layer, *, gate_col):
    rows, d = x.shape
    tm = _pick_tile(rows, 1408, _BF16_ROWS)
    tn = _pick_tile(d, 512, V7X_LANES)
    gcol = gate_col * (d // tn)
    blk = tm * d * 2 + d * tn * 6 + tm * tn * 9
    return pl.pallas_call(
        _out_proj_kernel,
        grid=(d // tn, rows // tm),
        in_specs=[pl.BlockSpec((tm, d), lambda j, i: (i, 0)),
                  pl.BlockSpec((None, d, tn), lambda j, i: (layer, 0, j)),
                  pl.BlockSpec((tm, tn), lambda j, i: (i, j)),
                  pl.BlockSpec((tm // V7X_SUBLANES, tn), lambda j, i: (i, gcol + j))],
        out_specs=pl.BlockSpec((tm, tn), lambda j, i: (i, j)),
        out_shape=jax.ShapeDtypeStruct((rows, d), jnp.float32),
        compiler_params=_params(("arbitrary", "arbitrary"), blk),
        name="out_proj",
    )(z, w_out, x, modg)


def _first_argmax(x, axis):
    m = jnp.max(x, axis=axis, keepdims=True)
    idx = lax.broadcasted_iota(jnp.int32, x.shape, axis)
    first = jnp.min(jnp.where(x == m, idx, x.shape[axis]), axis=axis, keepdims=True)
    return m, first


def _router_kernel(x_ref, g_ref, sc_ref, sh_ref, rwt_ref, rb_ref, hp_ref, eid_ref, wt_ref, rank_ref, cnt_ref,
                   h_ref, *, n_experts, pack_chunk):
    tr = x_ref.shape[0]
    for r in range(0, tr, V7X_SUBLANES):
        h = _modulated_norm_rows(x_ref, g_ref, sc_ref, sh_ref, r, V7X_SUBLANES)
        h_ref[r:r + V7X_SUBLANES, :] = h
        hp_ref[r:r + V7X_SUBLANES, :] = _pack_rows(h, pack_chunk)
    per_group = n_experts // N_GROUPS
    logits = lax.dot_general(rwt_ref[...], h_ref[...], (((1,), (1,)), ((), ())),
                             precision=lax.Precision.HIGHEST,
                             preferred_element_type=jnp.float32)
    s = jax.nn.sigmoid(logits)
    sel = (s + rb_ref[...]).reshape(N_GROUPS, per_group, tr)
    m1, i1 = _first_argmax(sel, 1)
    e_iota = lax.broadcasted_iota(jnp.int32, sel.shape, 1)
    m2 = jnp.max(jnp.where(e_iota == i1, -jnp.inf, sel), axis=1, keepdims=True)
    _, grp = _first_argmax(m1 + m2, 0)
    g_iota = lax.broadcasted_iota(jnp.int32, sel.shape, 0)
    in_group = jnp.max(jnp.where(g_iota == grp, sel, -jnp.inf), axis=0, keepdims=True)
    _, j1 = _first_argmax(in_group, 1)
    j_iota = lax.broadcasted_iota(jnp.int32, in_group.shape, 1)
    _, j2 = _first_argmax(jnp.where(j_iota == j1, -jnp.inf, in_group), 1)
    e1 = (grp * per_group + j1).reshape(1, tr)
    e2 = (grp * per_group + j2).reshape(1, tr)
    x_iota = lax.broadcasted_iota(jnp.int32, s.shape, 0)
    hit1 = x_iota == e1
    hit2 = x_iota == e2
    w1 = jnp.sum(jnp.where(hit1, s, 0.0), axis=0, keepdims=True)
    w2 = jnp.sum(jnp.where(hit2, s, 0.0), axis=0, keepdims=True)
    tot = w1 + w2
    eid_ref[...] = jnp.concatenate([e1, e2], axis=0)
    wt_ref[...] = jnp.concatenate([w1 / tot, w2 / tot], axis=0)

    @pl.when(pl.program_id(0) == 0)
    def _():
        cnt_ref[...] = jnp.zeros(cnt_ref.shape, jnp.float32)

    one1 = jnp.where(hit1, 1.0, 0.0)
    one2 = jnp.where(hit2, 1.0, 0.0)
    before = (lax.broadcasted_iota(jnp.int32, (tr, tr), 0) <
              lax.broadcasted_iota(jnp.int32, (tr, tr), 1)).astype(MXU_DTYPE)
    pre1 = jnp.dot(_bf16(one1), before, preferred_element_type=jnp.float32)
    pre2 = jnp.dot(_bf16(one2), before, preferred_element_type=jnp.float32)
    tot1 = jnp.sum(one1, axis=1, keepdims=True)
    tot2 = jnp.sum(one2, axis=1, keepdims=True)
    base = cnt_ref[:, 0:1]
    r1 = jnp.sum(one1 * (base + pre1), axis=0, keepdims=True)
    r2 = jnp.sum(one2 * (base + tot1 + pre2), axis=0, keepdims=True)
    rank_ref[...] = jnp.concatenate([r1, r2], axis=0).astype(jnp.int32)
    cnt_ref[...] = cnt_ref[...] + (tot1 + tot2)


def _norm_router(x, g, modg, router_w, router_b, *, shift_col, scale_col):
    rows, d = x.shape
    n_experts = router_w.shape[1]
    tr = _pick_tile(rows, 256, V7X_LANES) if rows % V7X_LANES == 0 else rows
    tg = tr // V7X_SUBLANES
    kern = functools.partial(_router_kernel, n_experts=n_experts, pack_chunk=_exact_div(d, MOE_PHASE_STEPS))
    return pl.pallas_call(
        kern,
        grid=(rows // tr,),
        in_specs=[pl.BlockSpec((tr, d), lambda i: (i, 0)),
                  pl.BlockSpec((1, d), lambda i: (0, 0)),
                  pl.BlockSpec((tg, d), lambda i: (i, scale_col)),
                  pl.BlockSpec((tg, d), lambda i: (i, shift_col)),
                  pl.BlockSpec((n_experts, d), lambda i: (0, 0)),
                  pl.BlockSpec((n_experts, 1), lambda i: (0, 0))],
        out_specs=[pl.BlockSpec((tr, d // 2), lambda i: (i, 0)),
                   pl.BlockSpec((TOP_K, tr), lambda i: (0, i)),
                   pl.BlockSpec((TOP_K, tr), lambda i: (0, i)),
                   pl.BlockSpec((TOP_K, tr), lambda i: (0, i)),
                   pl.BlockSpec((n_experts, V7X_LANES), lambda i: (0, 0))],
        scratch_shapes=[pltpu.VMEM((tr, d), jnp.float32)],
        out_shape=[jax.ShapeDtypeStruct((rows, d // 2), jnp.uint32),
                   jax.ShapeDtypeStruct((TOP_K, rows), jnp.int32),
                   jax.ShapeDtypeStruct((TOP_K, rows), jnp.float32),
                   jax.ShapeDtypeStruct((TOP_K, rows), jnp.int32),
                   jax.ShapeDtypeStruct((n_experts, V7X_LANES), jnp.float32)],
        compiler_params=_params(("arbitrary",), tr * d * 16 + n_experts * d * 4),
        name="norm_router",
    )(x, g.reshape(1, d), modg, modg, router_w.T, router_b.reshape(n_experts, 1))


def _dispatch_kernel(eid_ref, rank_ref, cnt_ref, dest_ref, bexp_ref, bcnt_ref, nused_ref, *, block_rows):
    n_experts = cnt_ref.shape[0]
    lanes = bexp_ref.shape[1]
    counts = cnt_ref[...]
    padded = jnp.floor((counts + (block_rows - 0.5)) * (1.0 / block_rows)) * block_rows
    upto = (lax.broadcasted_iota(jnp.int32, (n_experts, n_experts), 1) <=
            lax.broadcasted_iota(jnp.int32, (n_experts, n_experts), 0)).astype(jnp.float32)
    pend = jnp.dot(upto, padded, precision=lax.Precision.HIGHEST, preferred_element_type=jnp.float32)
    poff = pend - padded
    e_iota = lax.broadcasted_iota(jnp.int32, (n_experts, eid_ref.shape[1]), 0)
    for k in range(TOP_K):
        off = jnp.sum(jnp.where(e_iota == eid_ref[k:k + 1, :], poff[:, 0:1], 0.0), axis=0, keepdims=True)
        dest_ref[k:k + 1, :] = off.astype(jnp.int32) + rank_ref[k:k + 1, :]
    n_used = jnp.floor((jnp.max(pend, axis=0, keepdims=True)[:, 0:1] + 0.5) * (1.0 / block_rows))
    b_all = lax.broadcasted_iota(jnp.int32, (1, lanes), 1).astype(jnp.float32)
    start = jnp.minimum(b_all, n_used - 1.0) * block_rows
    ends_before = jnp.where(pend[:, 0:1] <= start, 1.0, 0.0)
    bexp = jnp.minimum(jnp.sum(ends_before, axis=0, keepdims=True), n_experts - 1.0)
    x_iota = lax.broadcasted_iota(jnp.int32, (n_experts, lanes), 0).astype(jnp.float32)
    last_row = jnp.sum(jnp.where(x_iota == bexp, (poff + counts)[:, 0:1], 0.0), axis=0, keepdims=True)
    filled = jnp.clip(last_row - b_all * block_rows, 0.0, float(block_rows))
    bexp_ref[...] = bexp.astype(jnp.int32)
    bcnt_ref[...] = jnp.where(b_all < n_used, filled, 0.0).astype(jnp.int32)
    nused_ref[...] = jnp.broadcast_to(n_used, nused_ref.shape).astype(jnp.int32)


def _dispatch_plan(eid, rank, counts, block_rows):
    top_k, rows = eid.shape
    n_experts = counts.shape[0]
    n_assign = top_k * rows
    n_blocks = -(-(n_assign + n_experts * (block_rows - 1)) // block_rows)
    lanes = -(-n_blocks // V7X_LANES) * V7X_LANES
    tr = _pick_tile(rows, 1536, V7X_LANES) if rows % V7X_LANES == 0 else rows
    kern = functools.partial(_dispatch_kernel, block_rows=block_rows)
    small = lambda n: pl.BlockSpec((1, n), lambda i: (0, 0))
    dest, blk_exp, blk_cnt, n_used = pl.pallas_call(
        kern,
        grid=(rows // tr,),
        in_specs=[pl.BlockSpec((top_k, tr), lambda i: (0, i)),
                  pl.BlockSpec((top_k, tr), lambda i: (0, i)),
                  pl.BlockSpec((n_experts, V7X_LANES), lambda i: (0, 0))],
        out_specs=[pl.BlockSpec((top_k, tr), lambda i: (0, i)), small(lanes), small(lanes), small(V7X_LANES)],
        out_shape=[jax.ShapeDtypeStruct((top_k, rows), jnp.int32),
                   jax.ShapeDtypeStruct((1, lanes), jnp.int32),
                   jax.ShapeDtypeStruct((1, lanes), jnp.int32),
                   jax.ShapeDtypeStruct((1, V7X_LANES), jnp.int32)],
        compiler_params=_params(("arbitrary",), n_experts * tr * 8),
        name="moe_dispatch",
    )(eid, rank, counts)
    tok = jnp.tile(jnp.arange(rows, dtype=jnp.int32), top_k)
    row_tok = jnp.zeros((n_blocks * block_rows,), jnp.int32).at[dest.reshape(n_assign)].set(tok)
    return (row_tok.reshape(n_blocks, 1, block_rows), dest, blk_exp[0, :n_blocks], blk_cnt[0, :n_blocks],
            n_used[0, :1])


def _experts_kernel(exp_ref, cnt_ref, nused_ref, tok0_ref, tok_next_ref, h_ref, wg_hbm, wu_hbm, wd_hbm,
                    o_ref, x_buf, sems, w_buf, w_sems, g_acc, u_acc, h_buf, *, layer):
    b = pl.program_id(0)
    n_used = nused_ref[0]
    used = b < n_used
    slot = b % 2
    p = MOE_PHASE_STEPS
    n_chunks = 3 * p
    d_e = w_buf.shape[1]

    def chunk_copy(blk, c):
        e = exp_ref[blk]
        if c < 2 * p:
            src = (wg_hbm if c % 2 == 0 else wu_hbm).at[layer, e, pl.ds((c // 2) * d_e, d_e), :]
        else:
            src = wd_hbm.at[layer, e, :, pl.ds((c - 2 * p) * d_e, d_e)]
        ring = c % MOE_WEIGHT_SLOTS
        return pltpu.make_async_copy(src, w_buf.at[ring], w_sems.at[ring])

    def row_copy(tok_ref, r, slot_):
        return pltpu.make_async_copy(h_ref.at[pl.ds(tok_ref[0, r], 1)], x_buf.at[slot_, pl.ds(r, 1)],
                                     sems.at[slot_])

    def gather_start(tok_ref, n, slot_):
        def body(grp, c):
            for u in range(DMA_ISSUE_UNROLL):
                row_copy(tok_ref, grp * DMA_ISSUE_UNROLL + u, slot_).start()
            return c

        lax.fori_loop(0, lax.shift_right_logical(n + (DMA_ISSUE_UNROLL - 1), DMA_ISSUE_SHIFT), body, 0)

    def gather_wait(tok_ref, n, slot_):
        def body(grp, c):
            for u in range(DMA_ISSUE_UNROLL):
                row_copy(tok_ref, grp * DMA_ISSUE_UNROLL + u, slot_).wait()
            return c

        lax.fori_loop(0, lax.shift_right_logical(n + (DMA_ISSUE_UNROLL - 1), DMA_ISSUE_SHIFT), body, 0)

    @pl.when(b == 0)
    def _():
        for c in range(MOE_WEIGHT_LOOKAHEAD):
            chunk_copy(0, c).start()
        x_buf[...] = jnp.zeros(x_buf.shape, x_buf.dtype)
        gather_start(tok0_ref, cnt_ref[0], 0)

    @pl.when(used)
    def _():
        gather_wait(tok0_ref, cnt_ref[b], slot)

        @pl.when(b + 1 < n_used)
        def _():
            gather_start(tok_next_ref, cnt_ref[b + 1], 1 - slot)

        for c in range(n_chunks):
            chunk_copy(b, c).wait()
            ahead = c + MOE_WEIGHT_LOOKAHEAD
            if ahead < n_chunks:
                chunk_copy(b, ahead).start()
            else:
                @pl.when(b + 1 < n_used)
                def _(ahead=ahead):
                    chunk_copy(b + 1, ahead - n_chunks).start()

            w = _bf16(w_buf[c % MOE_WEIGHT_SLOTS])
            if c < 2 * p:
                k = c // 2
                x = _bf16(_unpack_rows(x_buf[slot, :, k * (d_e // 2):(k + 1) * (d_e // 2)], d_e))
                part = jnp.dot(x, w, preferred_element_type=jnp.float32)
                acc = g_acc if c % 2 == 0 else u_acc
                if k == 0:
                    acc[...] = part
                else:
                    acc[...] += part
                if c == 2 * p - 1:
                    g = g_acc[...]
                    h_buf[...] = _bf16(g * jax.nn.sigmoid(g) * u_acc[...])
            else:
                n = c - 2 * p
                y = jnp.dot(h_buf[...], w, preferred_element_type=jnp.float32)
                o_ref[:, n * (d_e // 2):(n + 1) * (d_e // 2)] = _pack_rows(y, d_e)


def _experts(h, row_tok, w_gate, w_up, w_down, layer, blk_exp, blk_cnt, n_used):
    d = 2 * h.shape[1]
    d_e = w_gate.shape[3]
    n_blocks, _, block_rows = row_tok.shape
    n_rows = n_blocks * block_rows
    if d_e * MOE_PHASE_STEPS != d or (3 * MOE_PHASE_STEPS) % MOE_WEIGHT_SLOTS:
        raise ValueError("expert weight chunks must be square and fill the ring a whole number of times")
    if block_rows % DMA_ISSUE_UNROLL:
        raise ValueError("expert block rows must be a whole number of DMA issue groups")

    def out_map(b, exp, cnt, nu):
        return (jnp.maximum(jnp.minimum(b, nu[0] - 1), 0), 0)

    def tok_next_map(b, exp, cnt, nu):
        return (jnp.maximum(jnp.minimum(b + 1, nu[0] - 1), 0), 0, 0)

    grid_spec = pltpu.PrefetchScalarGridSpec(
        num_scalar_prefetch=3,
        grid=(n_blocks,),
        in_specs=[pl.BlockSpec((None, 1, block_rows), lambda b, exp, cnt, nu: (0, 0, 0), memory_space=pltpu.SMEM),
                  pl.BlockSpec((None, 1, block_rows), tok_next_map, memory_space=pltpu.SMEM),
                  pl.BlockSpec(memory_space=pl.ANY),
                  pl.BlockSpec(memory_space=pl.ANY),
                  pl.BlockSpec(memory_space=pl.ANY),
                  pl.BlockSpec(memory_space=pl.ANY)],
        out_specs=pl.BlockSpec((block_rows, d // 2), out_map),
        scratch_shapes=[pltpu.VMEM((2, block_rows, d // 2), jnp.uint32),
                        pltpu.SemaphoreType.DMA((2,)),
                        pltpu.VMEM((MOE_WEIGHT_SLOTS, d_e, d_e), jnp.float32),
                        pltpu.SemaphoreType.DMA((MOE_WEIGHT_SLOTS,)),
                        pltpu.VMEM((block_rows, d_e), jnp.float32),
                        pltpu.VMEM((block_rows, d_e), jnp.float32),
                        pltpu.VMEM((block_rows, d_e), MXU_DTYPE)],
    )
    blk_bytes = (block_rows * d * 2 * 2 + MOE_WEIGHT_SLOTS * d_e * d_e * 2 + d_e * d_e * 2 + block_rows * d_e * 9)
    return pl.pallas_call(
        functools.partial(_experts_kernel, layer=layer),
        grid_spec=grid_spec,
        out_shape=jax.ShapeDtypeStruct((n_rows, d // 2), jnp.uint32),
        compiler_params=_params(("arbitrary",), blk_bytes),
        name="moe_experts",
    )(blk_exp, blk_cnt, n_used, row_tok, row_tok, h, w_gate, w_up, w_down)


def _combine_kernel(dest_ref, y_ref, x_ref, wt_ref, g_ref, *refs, tr, total_rows, prompt_steps, pack_chunk):
    out_refs, (ybuf, sems) = refs[:-2], refs[-2:]
    i = pl.program_id(0)
    slot = i % 2

    def copy(step, slot_, k, r):
        row = dest_ref[k * total_rows + step * tr + r]
        return pltpu.make_async_copy(y_ref.at[pl.ds(row, 1)], ybuf.at[slot_, k, pl.ds(r, 1)], sems.at[slot_])

    def gather_start(step, slot_):
        def body(grp, c):
            for u in range(DMA_ISSUE_UNROLL):
                for k in range(TOP_K):
                    copy(step, slot_, k, grp * DMA_ISSUE_UNROLL + u).start()
            return c

        lax.fori_loop(0, tr // DMA_ISSUE_UNROLL, body, 0)

    @pl.when(i == 0)
    def _():
        gather_start(0, 0)

    @pl.when(i + 1 < pl.num_programs(0))
    def _():
        gather_start(i + 1, 1 - slot)

    def wait(grp, c):
        for u in range(DMA_ISSUE_UNROLL):
            for k in range(TOP_K):
                copy(i, slot, k, grp * DMA_ISSUE_UNROLL + u).wait()
        return c

    lax.fori_loop(0, tr // DMA_ISSUE_UNROLL, wait, 0)

    def emit(o_ref):
        for r in range(0, tr, V7X_SUBLANES):
            gi = r // V7X_SUBLANES
            rs = slice(r, r + V7X_SUBLANES)
            y = (wt_ref[rs, 0:1] * _unpack_rows(ybuf[slot, 0, rs, :], pack_chunk)
                 + wt_ref[rs, 1:2] * _unpack_rows(ybuf[slot, 1, rs, :], pack_chunk))
            o_ref[rs, :] = x_ref[rs, :] + g_ref[gi:gi + 1, :] * y

    if prompt_steps is None:
        emit(out_refs[0])
    else:
        pl.when(i < prompt_steps)(lambda: emit(out_refs[0]))
        pl.when(i >= prompt_steps)(lambda: emit(out_refs[1]))


def _combine(yb, dest, wts, x, modg, *, gate_col, split_rows=None):
    rows, d = x.shape
    if split_rows is None:
        tr = _pick_tile(rows, 128, V7X_SUBLANES)
        p_steps = None
        out_specs = pl.BlockSpec((tr, d), lambda i, dst: (i, 0))
        out_shape = jax.ShapeDtypeStruct((rows, d), jnp.float32)
    else:
        tr = _pick_tile(math.gcd(split_rows, rows - split_rows), 128, V7X_SUBLANES)
        p_steps = split_rows // tr
        out_specs = [pl.BlockSpec((tr, d), lambda i, dst: (jnp.minimum(i, p_steps - 1), 0)),
                     pl.BlockSpec((tr, d), lambda i, dst: (jnp.maximum(i - p_steps, 0), 0))]
        out_shape = [jax.ShapeDtypeStruct((split_rows, d), jnp.float32),
                     jax.ShapeDtypeStruct((rows - split_rows, d), jnp.float32)]
    kern = functools.partial(_combine_kernel, tr=tr, total_rows=rows, prompt_steps=p_steps,
                             pack_chunk=_exact_div(d, MOE_PHASE_STEPS))
    grid_spec = pltpu.PrefetchScalarGridSpec(
        num_scalar_prefetch=1,
        grid=(rows // tr,),
        in_specs=[pl.BlockSpec(memory_space=pl.ANY),
                  pl.BlockSpec((tr, d), lambda i, dst: (i, 0)),
                  pl.BlockSpec((tr, TOP_K), lambda i, dst: (i, 0)),
                  pl.BlockSpec((tr // V7X_SUBLANES, d), lambda i, dst: (i, gate_col))],
        out_specs=out_specs,
        scratch_shapes=[pltpu.VMEM((2, TOP_K, tr, d // 2), yb.dtype),
                        pltpu.SemaphoreType.DMA((2,))],
    )
    return pl.pallas_call(
        kern,
        grid_spec=grid_spec,
        out_shape=out_shape,
        compiler_params=_params(("arbitrary",), tr * d * 20),
        name="moe_combine",
    )(dest.reshape(-1), yb, x, wts.T, modg)


def kernel(x_prompt, x_sample, cache_k, cache_v, state_conv, c_prompt, c_sample, router_w, router_b, norm1_g, norm2_g, w_ada, b_ada, w_in, q_norm_g, k_norm_g, attn_sink, w_ao, dw_w, dw_b, cn_g, cn_b, w_co, w_out, moe_w_gate, moe_w_up, moe_w_down):
    batch, seq, d = x_prompt.shape
    dec_batch, t_new, _ = x_sample.shape
    depth = w_in.shape[0]
    window = cache_k.shape[2]
    n_kv = cache_k.shape[3]
    d_kv = n_kv * cache_k.shape[4]
    d_q = w_ao.shape[1]
    d_conv = w_co.shape[1]
    n_experts = router_w.shape[1]
    if q_norm_g.shape[1] != HEAD_DIM or cache_k.shape[4] != HEAD_DIM or dw_w.shape[1] != CONV_WIDTH:
        raise ValueError("unsupported head / conv geometry")
    if d_q != Q_PER_KV * d_kv or n_experts % N_GROUPS:
        raise ValueError("unsupported head / expert grouping")
    if seq % window or t_new % V7X_SUBLANES or t_new > window:
        raise ValueError("unsupported sequence geometry")
    prompt_rows = batch * seq
    sample_rows = dec_batch * t_new
    rows = prompt_rows + sample_rows
    ga_off = d_q + 2 * d_kv + 2 * d_conv

    SH1, SC1, G1, SH2, SC2, G2 = range(6)

    kp, vp, up, ks, vs, us = [], [], [], [], [], []
    for l in range(depth):
        mg = _adaln(c_prompt, c_sample, w_ada, b_ada, l, prompt_groups=seq // V7X_SUBLANES,
                    sample_groups=t_new // V7X_SUBLANES)
        if l == 0:
            h, x = _norm_mod_stack(x_prompt.reshape(prompt_rows, d), x_sample.reshape(sample_rows, d),
                                   norm1_g[l], mg, SH1, SC1)
        else:
            h = _norm_mod(x, norm1_g[l], mg, SH1, SC1)
        proj = _in_proj(h, w_in, l)

        a, k_last, v_last = _attn_prompt(proj, attn_sink[l], q_norm_g[l], k_norm_g[l], batch=batch, seq=seq,
                                         total_rows=rows, d_q=d_q, d_kv=d_kv, window=window)
        a, k_buf, v_buf = _attn_sample(a, proj, cache_k, cache_v, l, attn_sink[l], q_norm_g[l], k_norm_g[l],
                                       prompt_rows=prompt_rows, dec_batch=dec_batch, t_new=t_new,
                                       d_q=d_q, d_kv=d_kv, window=window)
        cv, u_last = _conv_prompt(proj, dw_w[l], dw_b[l], cn_g[l], cn_b[l], batch=batch, seq=seq,
                                  total_rows=rows, d_q=d_q, d_kv=d_kv, d_conv=d_conv)
        cv, u_new = _conv_sample(cv, proj, state_conv, l, dw_w[l], dw_b[l], cn_g[l], cn_b[l],
                                 prompt_rows=prompt_rows, dec_batch=dec_batch, t_new=t_new,
                                 d_q=d_q, d_kv=d_kv, d_conv=d_conv)

        z = _merge(a, cv, proj, w_ao, w_co, l, ga_off=ga_off)
        x = _out_proj(z, w_out, x, mg, l, gate_col=G1)

        h2, eid, wts, rank, counts = _norm_router(x, norm2_g[l], mg, router_w, router_b,
                                                  shift_col=SH2, scale_col=SC2)
        row_tok, dest, blk_exp, blk_cnt, n_used = _dispatch_plan(eid, rank, counts, MOE_BLOCK_ROWS)
        yb = _experts(h2, row_tok, moe_w_gate, moe_w_up, moe_w_down, l, blk_exp, blk_cnt, n_used)
        if l + 1 < depth:
            x = _combine(yb, dest, wts, x, mg, gate_col=G2)
        else:
            y_prompt, y_sample = _combine(yb, dest, wts, x, mg, gate_col=G2, split_rows=prompt_rows)

        kp.append(k_last)
        vp.append(v_last)
        up.append(u_last)
        ks.append(k_buf)
        vs.append(v_buf)
        us.append(u_new)

    def heads(bufs, n_seq):
        return jnp.stack(bufs).reshape(depth, n_seq, window, n_kv, HEAD_DIM)

    y_prompt = y_prompt.reshape(batch, seq, d)
    y_sample = y_sample.reshape(dec_batch, t_new, d)
    return (y_prompt, y_sample, heads(kp, batch), heads(vp, batch), jnp.stack(up),
            heads(ks, dec_batch), heads(vs, dec_batch), jnp.stack(us))
```

```python
import functools
import math

import jax
import jax.numpy as jnp
from jax import lax
from jax.experimental import pallas as pl
from jax.experimental.pallas import tpu as pltpu

EPS = 1e-6
HEAD_DIM = 128
Q_PER_KV = 4
CONV_WIDTH = 31
N_GROUPS = 8
TOP_K = 2

V7X_SUBLANES = 8
V7X_LANES = 128
V7X_VMEM_BYTES = 64 * 1024 * 1024
_MIB = 1024 * 1024
_BF16_ROWS = 2 * V7X_SUBLANES
MXU_DTYPE = jnp.bfloat16

CONV_HALO = 32
CONV_LEAD = CONV_HALO - (CONV_WIDTH - 1)
CONV_ROW_CHUNK = 64
CONV_CH_CHUNK = 256
MOE_BLOCK_ROWS = 448
MOE_PHASE_STEPS = 4
MOE_WEIGHT_SLOTS = 4
MOE_WEIGHT_LOOKAHEAD = MOE_WEIGHT_SLOTS - 1
DMA_ISSUE_SHIFT = 3
DMA_ISSUE_UNROLL = 1 << DMA_ISSUE_SHIFT


def _bf16(x):
    return x.astype(MXU_DTYPE)


def _pack_rows(x, chunk):
    half = chunk // 2
    out = []
    for c0 in range(0, x.shape[1], chunk):
        lo = lax.bitcast_convert_type(x[:, c0:c0 + half].astype(jnp.bfloat16).astype(jnp.float32), jnp.uint32)
        hi = lax.bitcast_convert_type(x[:, c0 + half:c0 + chunk].astype(jnp.bfloat16).astype(jnp.float32),
                                      jnp.uint32)
        out.append(lax.shift_right_logical(lo, jnp.uint32(16)) | hi)
    return out[0] if len(out) == 1 else jnp.concatenate(out, axis=1)


def _unpack_rows(p, chunk):
    half = chunk // 2
    out = []
    for c0 in range(0, p.shape[1], half):
        w = p[:, c0:c0 + half]
        out.append(lax.bitcast_convert_type(lax.shift_left(w, jnp.uint32(16)), jnp.float32))
        out.append(lax.bitcast_convert_type(w & jnp.uint32(0xFFFF0000), jnp.float32))
    return jnp.concatenate(out, axis=1)


def _pick_tile(n, target, align):
    best = None
    for t in range(align, min(n, target) + 1, align):
        if n % t == 0:
            best = t
    if best is None:
        raise ValueError(f"no tile for n={n} target={target} align={align}")
    return best


def _exact_div(a, b):
    q, r = divmod(a, b)
    if r:
        raise ValueError(f"{a} is not a multiple of {b}")
    return q


def _params(semantics, block_bytes):
    need = 2 * block_bytes + 8 * _MIB
    limit = int(min(max(need, 16 * _MIB), V7X_VMEM_BYTES - 8 * _MIB))
    return pltpu.CompilerParams(dimension_semantics=semantics, vmem_limit_bytes=limit)


def _adaln_kernel(c_ref, w_ref, b_ref, o_ref, *, batch, sample_row0, dec_batch, prompt_groups, sample_groups):
    c = c_ref[...]
    a = _bf16(c * jax.nn.sigmoid(c))
    mod = jnp.dot(a, _bf16(w_ref[...]), preferred_element_type=jnp.float32) + b_ref[...]
    tn = mod.shape[1]
    for i in range(batch):
        o_ref[i * prompt_groups:(i + 1) * prompt_groups, :] = jnp.broadcast_to(mod[i:i + 1, :], (prompt_groups, tn))
    base = batch * prompt_groups
    if sample_groups == 1:
        o_ref[base:base + dec_batch, :] = mod[sample_row0:sample_row0 + dec_batch, :]
    else:
        for j in range(dec_batch):
            o_ref[base + j * sample_groups:base + (j + 1) * sample_groups, :] = jnp.broadcast_to(
                mod[sample_row0 + j:sample_row0 + j + 1, :], (sample_groups, tn))


def _adaln(c_prompt, c_sample, w_ada, b_ada, layer, *, prompt_groups, sample_groups):
    depth, d, n = w_ada.shape
    batch, dec_batch = c_prompt.shape[0], c_sample.shape[0]
    sample_row0 = -(-batch // V7X_SUBLANES) * V7X_SUBLANES
    c_all = jnp.concatenate([c_prompt, jnp.zeros((sample_row0 - batch, d), jnp.float32), c_sample,
                             jnp.zeros((-dec_batch % V7X_SUBLANES, d), jnp.float32)], axis=0)
    rows = c_all.shape[0]
    groups = batch * prompt_groups + dec_batch * sample_groups
    tn = _pick_tile(n, 1024, V7X_LANES)
    kern = functools.partial(_adaln_kernel, batch=batch, sample_row0=sample_row0, dec_batch=dec_batch,
                             prompt_groups=prompt_groups, sample_groups=sample_groups)
    return pl.pallas_call(
        kern,
        grid=(n // tn,),
        in_specs=[pl.BlockSpec((rows, d), lambda j: (0, 0)),
                  pl.BlockSpec((None, d, tn), lambda j: (layer, 0, j)),
                  pl.BlockSpec((None, 1, tn), lambda j: (layer, 0, j))],
        out_specs=pl.BlockSpec((groups, tn), lambda j: (0, j)),
        out_shape=jax.ShapeDtypeStruct((groups, n), jnp.float32),
        compiler_params=_params(("arbitrary",), d * tn * 6 + rows * d * 4 + groups * tn * 4),
        name="adaln",
    )(c_all, w_ada, b_ada.reshape(depth, 1, n))


def _modulated_norm_rows(x_ref, g_ref, sc_ref, sh_ref, row0, n_rows):
    g = g_ref[...]
    parts = []
    for r in range(row0, row0 + n_rows, V7X_SUBLANES):
        x = x_ref[r:r + V7X_SUBLANES, :]
        gi = r // V7X_SUBLANES
        y = x * lax.rsqrt(jnp.mean(x * x, axis=-1, keepdims=True) + EPS) * g
        parts.append(y * (1.0 + sc_ref[gi:gi + 1, :]) + sh_ref[gi:gi + 1, :])
    return parts[0] if len(parts) == 1 else jnp.concatenate(parts, axis=0)


def _norm_mod_kernel(x_ref, g_ref, sc_ref, sh_ref, o_ref):
    for r in range(0, o_ref.shape[0], _BF16_ROWS):
        o_ref[r:r + _BF16_ROWS, :] = _bf16(_modulated_norm_rows(x_ref, g_ref, sc_ref, sh_ref, r, _BF16_ROWS))


def _norm_mod_stack_kernel(xp_ref, xs_ref, g_ref, sc_ref, sh_ref, o_ref, x_ref, *, prompt_steps):
    def emit(src_ref):
        x_ref[...] = src_ref[...]
        _norm_mod_kernel(src_ref, g_ref, sc_ref, sh_ref, o_ref)

    is_prompt = pl.program_id(0) < prompt_steps
    pl.when(is_prompt)(lambda: emit(xp_ref))
    pl.when(jnp.logical_not(is_prompt))(lambda: emit(xs_ref))


def _norm_mod_stack(x_prompt, x_sample, g, modg, shift_col, scale_col):
    p_rows, d = x_prompt.shape
    s_rows = x_sample.shape[0]
    rows = p_rows + s_rows
    tr = _pick_tile(math.gcd(p_rows, s_rows), 256, _BF16_ROWS)
    tg = tr // V7X_SUBLANES
    p_steps = p_rows // tr
    kern = functools.partial(_norm_mod_stack_kernel, prompt_steps=p_steps)
    return pl.pallas_call(
        kern,
        grid=(rows // tr,),
        in_specs=[pl.BlockSpec((tr, d), lambda i: (jnp.minimum(i, p_steps - 1), 0)),
                  pl.BlockSpec((tr, d), lambda i: (jnp.maximum(i - p_steps, 0), 0)),
                  pl.BlockSpec((1, d), lambda i: (0, 0)),
                  pl.BlockSpec((tg, d), lambda i: (i, scale_col)),
                  pl.BlockSpec((tg, d), lambda i: (i, shift_col))],
        out_specs=[pl.BlockSpec((tr, d), lambda i: (i, 0)),
                   pl.BlockSpec((tr, d), lambda i: (i, 0))],
        out_shape=[jax.ShapeDtypeStruct((rows, d), MXU_DTYPE),
                   jax.ShapeDtypeStruct((rows, d), jnp.float32)],
        compiler_params=_params(("arbitrary",), tr * d * 16),
        name="norm_mod_stack",
    )(x_prompt, x_sample, g.reshape(1, d), modg, modg)


def _matmul_kernel(x_ref, w_ref, o_ref):
    o_ref[...] = jnp.dot(x_ref[...], _bf16(w_ref[...]), preferred_element_type=jnp.float32)


def _in_proj(h, w_in, layer):
    rows, k = h.shape
    n = w_in.shape[2]
    tm = _pick_tile(rows, 1408, _BF16_ROWS)
    tn = _pick_tile(n, 512, V7X_LANES)
    return pl.pallas_call(
        _matmul_kernel,
        grid=(n // tn, rows // tm),
        in_specs=[pl.BlockSpec((tm, k), lambda j, i: (i, 0)),
                  pl.BlockSpec((None, k, tn), lambda j, i: (layer, 0, j))],
        out_specs=pl.BlockSpec((tm, tn), lambda j, i: (i, j)),
        out_shape=jax.ShapeDtypeStruct((rows, n), jnp.float32),
        compiler_params=_params(("arbitrary", "arbitrary"), tm * k * 2 + k * tn * 6 + tm * tn * 4),
        name="in_proj",
    )(h, w_in)


def _head_rms(x, g):
    return x * lax.rsqrt(jnp.mean(x * x, axis=-1, keepdims=True) + EPS) * g


def _window_mask(n_q, window, older_present):
    n_rows = Q_PER_KV * n_q
    row = lax.broadcasted_iota(jnp.int32, (n_rows, 2 * window), 0) % n_q
    col = lax.broadcasted_iota(jnp.int32, (n_rows, 2 * window), 1)
    return (col >= row) & (col <= row + window) & ((col >= window) | older_present)


def _kv_group_attention(q_heads, k_all, v_all, sink_vals, valid, n_q):
    q_all = jnp.concatenate(q_heads, axis=0)
    sink_col = jnp.concatenate([jnp.full((n_q, 1), sv, jnp.float32) for sv in sink_vals], axis=0)
    s = lax.dot_general(_bf16(q_all), _bf16(k_all), (((1,), (1,)), ((), ())),
                        preferred_element_type=jnp.float32) * (1.0 / math.sqrt(HEAD_DIM))
    s = jnp.where(valid, s, -jnp.inf)
    m = jnp.maximum(jnp.max(s, axis=-1, keepdims=True), sink_col)
    p = jnp.exp(s - m)
    denom = jnp.sum(p, axis=-1, keepdims=True) + jnp.exp(sink_col - m)
    return jnp.dot(_bf16(p), _bf16(v_all), preferred_element_type=jnp.float32) / denom


def _attn_prompt_kernel(sink_ref, q_ref, kp_ref, kc_ref, vp_ref, vc_ref, qg_ref, kg_ref, o_ref, kn_ref,
                        vn_ref, *, n_kv, window):
    qg = qg_ref[...]
    kg = kg_ref[...]
    valid = _window_mask(window, window, pl.program_id(1) > 0)
    vn_ref[...] = vc_ref[...]
    for h in range(n_kv):
        ksl = slice(h * HEAD_DIM, (h + 1) * HEAD_DIM)
        k_cur = _head_rms(kc_ref[:, ksl], kg)
        kn_ref[:, ksl] = k_cur
        k_all = jnp.concatenate([_head_rms(kp_ref[:, ksl], kg), k_cur], axis=0)
        v_all = jnp.concatenate([vp_ref[:, ksl], vc_ref[:, ksl]], axis=0)
        heads = [h * Q_PER_KV + g for g in range(Q_PER_KV)]
        q_heads = [_head_rms(q_ref[:, hq * HEAD_DIM:(hq + 1) * HEAD_DIM], qg) for hq in heads]
        o = _kv_group_attention(q_heads, k_all, v_all, [sink_ref[hq] for hq in heads], valid, window)
        for g, hq in enumerate(heads):
            o_ref[:, hq * HEAD_DIM:(hq + 1) * HEAD_DIM] = _bf16(o[g * window:(g + 1) * window])


def _attn_prompt(proj, sink, q_g, k_g, *, batch, seq, total_rows, d_q, d_kv, window):
    n_kv = d_kv // HEAD_DIM
    nb = _exact_div(seq, window)
    k_col = _exact_div(d_q, d_kv)
    v_col = k_col + 1
    kern = functools.partial(_attn_prompt_kernel, n_kv=n_kv, window=window)

    def cur(b, i, s):
        return b * nb + i

    def prev(b, i, s):
        return b * nb + jnp.maximum(i - 1, 0)

    grid_spec = pltpu.PrefetchScalarGridSpec(
        num_scalar_prefetch=1,
        grid=(batch, nb),
        in_specs=[pl.BlockSpec((window, d_q), lambda b, i, s: (cur(b, i, s), 0)),
                  pl.BlockSpec((window, d_kv), lambda b, i, s: (prev(b, i, s), k_col)),
                  pl.BlockSpec((window, d_kv), lambda b, i, s: (cur(b, i, s), k_col)),
                  pl.BlockSpec((window, d_kv), lambda b, i, s: (prev(b, i, s), v_col)),
                  pl.BlockSpec((window, d_kv), lambda b, i, s: (cur(b, i, s), v_col)),
                  pl.BlockSpec((1, HEAD_DIM), lambda b, i, s: (0, 0)),
                  pl.BlockSpec((1, HEAD_DIM), lambda b, i, s: (0, 0))],
        out_specs=[pl.BlockSpec((window, d_q), lambda b, i, s: (cur(b, i, s), 0)),
                   pl.BlockSpec((None, window, d_kv), lambda b, i, s: (b, 0, 0)),
                   pl.BlockSpec((None, window, d_kv), lambda b, i, s: (b, 0, 0))],
    )
    return pl.pallas_call(
        kern,
        grid_spec=grid_spec,
        out_shape=[jax.ShapeDtypeStruct((total_rows, d_q), MXU_DTYPE),
                   jax.ShapeDtypeStruct((batch, window, d_kv), jnp.float32),
                   jax.ShapeDtypeStruct((batch, window, d_kv), jnp.float32)],
        compiler_params=_params(("arbitrary", "arbitrary"), window * (d_q * 6 + d_kv * 28)),
        name="attn_prompt",
    )(sink, proj, proj, proj, proj, proj, q_g.reshape(1, HEAD_DIM), k_g.reshape(1, HEAD_DIM))


def _attn_sample_kernel(sink_ref, a_in_ref, q_ref, kn_ref, vn_ref, kc_ref, vc_ref, qg_ref, kg_ref,
                        o_ref, kout_ref, vout_ref, *, n_kv, window, t_new, seqs):
    del a_in_ref
    qg = qg_ref[...]
    kg = kg_ref[...]
    valid = _window_mask(t_new, window, True)
    fill = jnp.zeros((window - t_new, HEAD_DIM), jnp.float32)
    keep = window - t_new
    for sq in range(seqs):
        rows = slice(sq * t_new, (sq + 1) * t_new)
        kout_ref[sq, 0:keep, :] = kc_ref[sq, t_new:window, :]
        vout_ref[sq, 0:keep, :] = vc_ref[sq, t_new:window, :]
        vout_ref[sq, keep:window, :] = vn_ref[rows, :]
        for h in range(n_kv):
            ksl = slice(h * HEAD_DIM, (h + 1) * HEAD_DIM)
            k_new = _head_rms(kn_ref[rows, ksl], kg)
            kout_ref[sq, keep:window, ksl] = k_new
            k_all = jnp.concatenate([kc_ref[sq, :, ksl], k_new, fill], axis=0)
            v_all = jnp.concatenate([vc_ref[sq, :, ksl], vn_ref[rows, ksl], fill], axis=0)
            heads = [h * Q_PER_KV + g for g in range(Q_PER_KV)]
            q_heads = [_head_rms(q_ref[rows, hq * HEAD_DIM:(hq + 1) * HEAD_DIM], qg) for hq in heads]
            o = _kv_group_attention(q_heads, k_all, v_all, [sink_ref[hq] for hq in heads], valid, t_new)
            for g, hq in enumerate(heads):
                o_ref[rows, hq * HEAD_DIM:(hq + 1) * HEAD_DIM] = _bf16(o[g * t_new:(g + 1) * t_new])


def _attn_sample(a, proj, cache_k, cache_v, layer, sink, q_g, k_g, *, prompt_rows, dec_batch, t_new,
                 d_q, d_kv, window):
    n_kv = d_kv // HEAD_DIM
    depth = cache_k.shape[0]
    seqs = _exact_div(_BF16_ROWS, t_new) if _BF16_ROWS % t_new == 0 and dec_batch % (_BF16_ROWS // t_new) == 0 else 1
    tr = seqs * t_new
    row0 = _exact_div(prompt_rows, tr)
    k_col = _exact_div(d_q, d_kv)
    kern = functools.partial(_attn_sample_kernel, n_kv=n_kv, window=window, t_new=t_new, seqs=seqs)
    kc = cache_k.reshape(depth, dec_batch, window, d_kv)
    vc = cache_v.reshape(depth, dec_batch, window, d_kv)
    buf_spec = pl.BlockSpec((seqs, window, d_kv), lambda i, s: (i, 0, 0))
    grid_spec = pltpu.PrefetchScalarGridSpec(
        num_scalar_prefetch=1,
        grid=(dec_batch // seqs,),
        in_specs=[pl.BlockSpec(memory_space=pl.ANY),
                  pl.BlockSpec((tr, d_q), lambda i, s: (row0 + i, 0)),
                  pl.BlockSpec((tr, d_kv), lambda i, s: (row0 + i, k_col)),
                  pl.BlockSpec((tr, d_kv), lambda i, s: (row0 + i, k_col + 1)),
                  pl.BlockSpec((None, seqs, window, d_kv), lambda i, s: (layer, i, 0, 0)),
                  pl.BlockSpec((None, seqs, window, d_kv), lambda i, s: (layer, i, 0, 0)),
                  pl.BlockSpec((1, HEAD_DIM), lambda i, s: (0, 0)),
                  pl.BlockSpec((1, HEAD_DIM), lambda i, s: (0, 0))],
        out_specs=[pl.BlockSpec((tr, d_q), lambda i, s: (row0 + i, 0)), buf_spec, buf_spec],
    )
    buf_shape = jax.ShapeDtypeStruct((dec_batch, window, d_kv), jnp.float32)
    return pl.pallas_call(
        kern,
        grid_spec=grid_spec,
        out_shape=[jax.ShapeDtypeStruct(a.shape, a.dtype), buf_shape, buf_shape],
        input_output_aliases={1: 0},
        compiler_params=_params(("arbitrary",), tr * (d_q * 6 + d_kv * 8) + seqs * window * d_kv * 16),
        name="attn_sample",
    )(sink, a, proj, proj, proj, kc, vc, q_g.reshape(1, HEAD_DIM), k_g.reshape(1, HEAD_DIM))


def _depthwise_conv(uext_ref, base, n_rows, w_ref, b_ref, conv_ref, conv_base, part_ref):
    d_conv = conv_ref.shape[-1]
    ch_chunk = part_ref.shape[-1]
    for c0 in range(0, d_conv, ch_chunk):
        cs = slice(c0, c0 + ch_chunk)
        for s in range(V7X_SUBLANES):
            span = n_rows + -(-(CONV_LEAD + s) // V7X_SUBLANES) * V7X_SUBLANES
            for t0 in range(0, span, CONV_ROW_CHUNK):
                rc = min(CONV_ROW_CHUNK, span - t0)
                acc = None
                for j in range(s, CONV_WIDTH, V7X_SUBLANES):
                    start = base + t0 + j - s
                    term = uext_ref[start:start + rc, cs] * w_ref[j:j + 1, cs]
                    acc = term if acc is None else acc + term
                part_ref[s, t0:t0 + rc, :] = acc
        for r0 in range(0, n_rows, CONV_ROW_CHUNK):
            rc = min(CONV_ROW_CHUNK, n_rows - r0)
            acc = jnp.broadcast_to(b_ref[:, cs], (rc, ch_chunk))
            for s in range(V7X_SUBLANES):
                off = r0 + CONV_LEAD + s
                acc = acc + part_ref[s, off:off + rc, :]
            conv_ref[conv_base + r0:conv_base + r0 + rc, cs] = acc


def _conv_ch_chunk(d_conv):
    return _pick_tile(d_conv, CONV_CH_CHUNK, V7X_LANES)


def _layer_norm_swish(x, g, b):
    mu = jnp.mean(x, axis=-1, keepdims=True)
    xc = x - mu
    var = jnp.mean(xc * xc, axis=-1, keepdims=True)
    y = xc * lax.rsqrt(var + EPS) * g + b
    return y * jax.nn.sigmoid(y)


def _conv_prompt_kernel(uv_ref, ug_ref, w_ref, b_ref, g_ref, bb_ref, o_ref, st_ref, uext_ref, conv_ref,
                        part_ref, *, tt):
    @pl.when(pl.program_id(1) == 0)
    def _():
        uext_ref[0:CONV_HALO, :] = jnp.zeros((CONV_HALO, uext_ref.shape[1]), jnp.float32)

    uext_ref[CONV_HALO:CONV_HALO + tt, :] = uv_ref[...] * jax.nn.sigmoid(ug_ref[...])
    _depthwise_conv(uext_ref, 0, tt, w_ref, b_ref, conv_ref, 0, part_ref)
    o_ref[...] = _bf16(_layer_norm_swish(conv_ref[...], g_ref[...], bb_ref[...]))
    st_ref[...] = uext_ref[tt + CONV_LEAD:tt + CONV_HALO, :]
    uext_ref[0:CONV_HALO, :] = uext_ref[tt:tt + CONV_HALO, :]


def _conv_prompt(proj, dw_w, dw_b, cn_g, cn_b, *, batch, seq, total_rows, d_q, d_kv, d_conv):
    tt = _pick_tile(seq, 256, 32)
    nt = seq // tt
    uv_col = _exact_div(d_q + 2 * d_kv, d_conv)
    kern = functools.partial(_conv_prompt_kernel, tt=tt)
    return pl.pallas_call(
        kern,
        grid=(batch, nt),
        in_specs=[pl.BlockSpec((tt, d_conv), lambda b, t: (b * nt + t, uv_col)),
                  pl.BlockSpec((tt, d_conv), lambda b, t: (b * nt + t, uv_col + 1)),
                  pl.BlockSpec((CONV_WIDTH, d_conv), lambda b, t: (0, 0)),
                  pl.BlockSpec((1, d_conv), lambda b, t: (0, 0)),
                  pl.BlockSpec((1, d_conv), lambda b, t: (0, 0)),
                  pl.BlockSpec((1, d_conv), lambda b, t: (0, 0))],
        out_specs=[pl.BlockSpec((tt, d_conv), lambda b, t: (b * nt + t, 0)),
                   pl.BlockSpec((None, CONV_WIDTH - 1, d_conv), lambda b, t: (b, 0, 0))],
        out_shape=[jax.ShapeDtypeStruct((total_rows, d_conv), MXU_DTYPE),
                   jax.ShapeDtypeStruct((batch, CONV_WIDTH - 1, d_conv), jnp.float32)],
        scratch_shapes=[pltpu.VMEM((tt + CONV_HALO, d_conv), jnp.float32),
                        pltpu.VMEM((tt, d_conv), jnp.float32),
                        pltpu.VMEM((V7X_SUBLANES, tt + 2 * V7X_SUBLANES, _conv_ch_chunk(d_conv)), jnp.float32)],
        compiler_params=_params(("arbitrary", "arbitrary"), tt * d_conv * 18),
        name="conv_prompt",
    )(proj, proj, dw_w, dw_b.reshape(1, d_conv), cn_g.reshape(1, d_conv), cn_b.reshape(1, d_conv))


def _conv_sample_kernel(cv_in_ref, uv_ref, ug_ref, st_ref, w_ref, b_ref, g_ref, bb_ref, o_ref, nst_ref,
                        uext_ref, conv_ref, part_ref, *, t_new, seqs):
    del cv_in_ref
    ext = CONV_HALO + t_new
    for sq in range(seqs):
        base = sq * ext
        rows = slice(sq * t_new, (sq + 1) * t_new)
        uext_ref[base:base + CONV_LEAD, :] = jnp.zeros((CONV_LEAD, uext_ref.shape[1]), jnp.float32)
        uext_ref[base + CONV_LEAD:base + CONV_HALO, :] = st_ref[sq]
        uext_ref[base + CONV_HALO:base + ext, :] = uv_ref[rows, :] * jax.nn.sigmoid(ug_ref[rows, :])
        nst_ref[sq] = uext_ref[base + ext - (CONV_WIDTH - 1):base + ext, :]
        _depthwise_conv(uext_ref, base, t_new, w_ref, b_ref, conv_ref, sq * t_new, part_ref)
    o_ref[...] = _bf16(_layer_norm_swish(conv_ref[...], g_ref[...], bb_ref[...]))


def _conv_sample(cv, proj, state_conv, layer, dw_w, dw_b, cn_g, cn_b, *, prompt_rows, dec_batch, t_new,
                 d_q, d_kv, d_conv):
    seqs = _exact_div(_BF16_ROWS, t_new) if _BF16_ROWS % t_new == 0 and dec_batch % (_BF16_ROWS // t_new) == 0 else 1
    tr = seqs * t_new
    row0 = _exact_div(prompt_rows, tr)
    uv_col = _exact_div(d_q + 2 * d_kv, d_conv)
    kern = functools.partial(_conv_sample_kernel, t_new=t_new, seqs=seqs)
    return pl.pallas_call(
        kern,
        grid=(dec_batch // seqs,),
        in_specs=[pl.BlockSpec(memory_space=pl.ANY),
                  pl.BlockSpec((tr, d_conv), lambda i: (row0 + i, uv_col)),
                  pl.BlockSpec((tr, d_conv), lambda i: (row0 + i, uv_col + 1)),
                  pl.BlockSpec((None, seqs, CONV_WIDTH - 1, d_conv), lambda i: (layer, i, 0, 0)),
                  pl.BlockSpec((CONV_WIDTH, d_conv), lambda i: (0, 0)),
                  pl.BlockSpec((1, d_conv), lambda i: (0, 0)),
                  pl.BlockSpec((1, d_conv), lambda i: (0, 0)),
                  pl.BlockSpec((1, d_conv), lambda i: (0, 0))],
        out_specs=[pl.BlockSpec((tr, d_conv), lambda i: (row0 + i, 0)),
                   pl.BlockSpec((seqs, CONV_WIDTH - 1, d_conv), lambda i: (i, 0, 0))],
        out_shape=[jax.ShapeDtypeStruct(cv.shape, cv.dtype),
                   jax.ShapeDtypeStruct((dec_batch, CONV_WIDTH - 1, d_conv), jnp.float32)],
        input_output_aliases={0: 0},
        scratch_shapes=[pltpu.VMEM((seqs * (CONV_HALO + t_new), d_conv), jnp.float32),
                        pltpu.VMEM((tr, d_conv), jnp.float32),
                        pltpu.VMEM((V7X_SUBLANES, t_new + 2 * V7X_SUBLANES, _conv_ch_chunk(d_conv)), jnp.float32)],
        compiler_params=_params(("arbitrary",), (tr * 12 + seqs * 64 * 4 + 40 * 4) * d_conv),
        name="conv_sample",
    )(cv, proj, proj, state_conv, dw_w, dw_b.reshape(1, d_conv), cn_g.reshape(1, d_conv), cn_b.reshape(1, d_conv))


def _merge_kernel(a_ref, cv_ref, wao_ref, wco_ref, ga_ref, gb_ref, o_ref):
    pa = jnp.dot(a_ref[...], _bf16(wao_ref[...]), preferred_element_type=jnp.float32)
    pc = jnp.dot(cv_ref[...], _bf16(wco_ref[...]), preferred_element_type=jnp.float32)
    o_ref[...] = _bf16(jax.nn.sigmoid(ga_ref[...]) * pa + jax.nn.sigmoid(gb_ref[...]) * pc)


def _merge(a, cv, proj, w_ao, w_co, layer, *, ga_off):
    rows, d_q = a.shape
    d_conv = cv.shape[1]
    d = w_ao.shape[2]
    tm = _pick_tile(rows, 384, _BF16_ROWS)
    tn = _pick_tile(d, 512, V7X_LANES)
    ga_col = _exact_div(ga_off, tn)
    gb_col = ga_col + d // tn
    blk = tm * (d_q + d_conv) * 2 + (d_q + d_conv) * tn * 6 + tm * tn * 10
    return pl.pallas_call(
        _merge_kernel,
        grid=(d // tn, rows // tm),
        in_specs=[pl.BlockSpec((tm, d_q), lambda j, i: (i, 0)),
                  pl.BlockSpec((tm, d_conv), lambda j, i: (i, 0)),
                  pl.BlockSpec((None, d_q, tn), lambda j, i: (layer, 0, j)),
                  pl.BlockSpec((None, d_conv, tn), lambda j, i: (layer, 0, j)),
                  pl.BlockSpec((tm, tn), lambda j, i: (i, ga_col + j)),
                  pl.BlockSpec((tm, tn), lambda j, i: (i, gb_col + j))],
        out_specs=pl.BlockSpec((tm, tn), lambda j, i: (i, j)),
        out_shape=jax.ShapeDtypeStruct((rows, d), MXU_DTYPE),
        compiler_params=_params(("arbitrary", "arbitrary"), blk),
        name="merge",
    )(a, cv, w_ao, w_co, proj, proj)


def _out_proj_kernel(z_ref, w_ref, x_ref, g_ref, o_ref):
    m = jnp.dot(z_ref[...], _bf16(w_ref[...]), preferred_element_type=jnp.float32)
    for r in range(0, m.shape[0], V7X_SUBLANES):
        gi = r // V7X_SUBLANES
        rs = slice(r, r + V7X_SUBLANES)
        o_ref[rs, :] = x_ref[rs, :] + g_ref[gi:gi + 1, :] * m[rs]


def _out_proj(z, w_out, x, modg, layer, *, gate_col):
    rows, d = x.shape
    tm = _pick_tile(rows, 1408, _BF16_ROWS)
    tn = _pick_tile(d, 512, V7X_LANES)
    gcol = gate_col * (d // tn)
    blk = tm * d * 2 + d * tn * 6 + tm * tn * 9
    return pl.pallas_call(
        _out_proj_kernel,
        grid=(d // tn, rows // tm),
        in_specs=[pl.BlockSpec((tm, d), lambda j, i: (i, 0)),
                  pl.BlockSpec((None, d, tn), lambda j, i: (layer, 0, j)),
                  pl.BlockSpec((tm, tn), lambda j, i: (i, j)),
                  pl.BlockSpec((tm // V7X_SUBLANES, tn), lambda j, i: (i, gcol + j))],
        out_specs=pl.BlockSpec((tm, tn), lambda j, i: (i, j)),
        out_shape=jax.ShapeDtypeStruct((rows, d), jnp.float32),
        compiler_params=_params(("arbitrary", "arbitrary"), blk),
        name="out_proj",
    )(z, w_out, x, modg)


def _first_argmax(x, axis):
    m = jnp.max(x, axis=axis, keepdims=True)
    idx = lax.broadcasted_iota(jnp.int32, x.shape, axis)
    first = jnp.min(jnp.where(x == m, idx, x.shape[axis]), axis=axis, keepdims=True)
    return m, first


def _router_kernel(x_ref, g_ref, sc_ref, sh_ref, rwt_ref, rb_ref, hp_ref, eid_ref, wt_ref, rank_ref, cnt_ref,
                   h_ref, *, n_experts, pack_chunk):
    tr = x_ref.shape[0]
    for r in range(0, tr, V7X_SUBLANES):
        h = _modulated_norm_rows(x_ref, g_ref, sc_ref, sh_ref, r, V7X_SUBLANES)
        h_ref[r:r + V7X_SUBLANES, :] = h
        hp_ref[r:r + V7X_SUBLANES, :] = _pack_rows(h, pack_chunk)
    per_group = n_experts // N_GROUPS
    h_all = h_ref[...]
    h_hi = _bf16(h_all)
    h_lo = _bf16(h_all - h_hi.astype(jnp.float32))
    rw = rwt_ref[...]
    rw_hi = _bf16(rw)
    rw_lo = _bf16(rw - rw_hi.astype(jnp.float32))
    nt = (((1,), (1,)), ((), ()))
    logits = (lax.dot_general(rw_hi, h_hi, nt, preferred_element_type=jnp.float32)
              + (lax.dot_general(rw_hi, h_lo, nt, preferred_element_type=jnp.float32)
                 + lax.dot_general(rw_lo, h_hi, nt, preferred_element_type=jnp.float32)))
    s = jax.nn.sigmoid(logits)
    sel = (s + rb_ref[...]).reshape(N_GROUPS, per_group, tr)
    m1, i1 = _first_argmax(sel, 1)
    e_iota = lax.broadcasted_iota(jnp.int32, sel.shape, 1)
    m2 = jnp.max(jnp.where(e_iota == i1, -jnp.inf, sel), axis=1, keepdims=True)
    _, grp = _first_argmax(m1 + m2, 0)
    g_iota = lax.broadcasted_iota(jnp.int32, sel.shape, 0)
    in_group = jnp.max(jnp.where(g_iota == grp, sel, -jnp.inf), axis=0, keepdims=True)
    _, j1 = _first_argmax(in_group, 1)
    j_iota = lax.broadcasted_iota(jnp.int32, in_group.shape, 1)
    _, j2 = _first_argmax(jnp.where(j_iota == j1, -jnp.inf, in_group), 1)
    e1 = (grp * per_group + j1).reshape(1, tr)
    e2 = (grp * per_group + j2).reshape(1, tr)
    x_iota = lax.broadcasted_iota(jnp.int32, s.shape, 0)
    hit1 = x_iota == e1
    hit2 = x_iota == e2
    w1 = jnp.sum(jnp.where(hit1, s, 0.0), axis=0, keepdims=True)
    w2 = jnp.sum(jnp.where(hit2, s, 0.0), axis=0, keepdims=True)
    tot = w1 + w2
    eid_ref[...] = jnp.concatenate([e1, e2], axis=0)
    wt_ref[...] = jnp.concatenate([w1 / tot, w2 / tot], axis=0)

    @pl.when(pl.program_id(0) == 0)
    def _():
        cnt_ref[...] = jnp.zeros(cnt_ref.shape, jnp.float32)

    one1 = jnp.where(hit1, 1.0, 0.0)
    one2 = jnp.where(hit2, 1.0, 0.0)
    before = (lax.broadcasted_iota(jnp.int32, (tr, tr), 0) <
              lax.broadcasted_iota(jnp.int32, (tr, tr), 1)).astype(MXU_DTYPE)
    pre1 = jnp.dot(_bf16(one1), before, preferred_element_type=jnp.float32)
    pre2 = jnp.dot(_bf16(one2), before, preferred_element_type=jnp.float32)
    tot1 = jnp.sum(one1, axis=1, keepdims=True)
    tot2 = jnp.sum(one2, axis=1, keepdims=True)
    base = cnt_ref[:, 0:1]
    r1 = jnp.sum(one1 * (base + pre1), axis=0, keepdims=True)
    r2 = jnp.sum(one2 * (base + tot1 + pre2), axis=0, keepdims=True)
    rank_ref[...] = jnp.concatenate([r1, r2], axis=0).astype(jnp.int32)
    cnt_ref[...] = cnt_ref[...] + (tot1 + tot2)


def _norm_router(x, g, modg, router_w, router_b, *, shift_col, scale_col):
    rows, d = x.shape
    n_experts = router_w.shape[1]
    tr = _pick_tile(rows, 256, V7X_LANES) if rows % V7X_LANES == 0 else rows
    tg = tr // V7X_SUBLANES
    kern = functools.partial(_router_kernel, n_experts=n_experts, pack_chunk=_exact_div(d, MOE_PHASE_STEPS))
    return pl.pallas_call(
        kern,
        grid=(rows // tr,),
        in_specs=[pl.BlockSpec((tr, d), lambda i: (i, 0)),
                  pl.BlockSpec((1, d), lambda i: (0, 0)),
                  pl.BlockSpec((tg, d), lambda i: (i, scale_col)),
                  pl.BlockSpec((tg, d), lambda i: (i, shift_col)),
                  pl.BlockSpec((n_experts, d), lambda i: (0, 0)),
                  pl.BlockSpec((n_experts, 1), lambda i: (0, 0))],
        out_specs=[pl.BlockSpec((tr, d // 2), lambda i: (i, 0)),
                   pl.BlockSpec((TOP_K, tr), lambda i: (0, i)),
                   pl.BlockSpec((TOP_K, tr), lambda i: (0, i)),
                   pl.BlockSpec((TOP_K, tr), lambda i: (0, i)),
                   pl.BlockSpec((n_experts, V7X_LANES), lambda i: (0, 0))],
        scratch_shapes=[pltpu.VMEM((tr, d), jnp.float32)],
        out_shape=[jax.ShapeDtypeStruct((rows, d // 2), jnp.uint32),
                   jax.ShapeDtypeStruct((TOP_K, rows), jnp.int32),
                   jax.ShapeDtypeStruct((TOP_K, rows), jnp.float32),
                   jax.ShapeDtypeStruct((TOP_K, rows), jnp.int32),
                   jax.ShapeDtypeStruct((n_experts, V7X_LANES), jnp.float32)],
        compiler_params=_params(("arbitrary",), tr * d * 16 + n_experts * d * 4),
        name="norm_router",
    )(x, g.reshape(1, d), modg, modg, router_w.T, router_b.reshape(n_experts, 1))


def _dispatch_kernel(eid_ref, rank_ref, cnt_ref, dest_ref, bexp_ref, bcnt_ref, nused_ref, *, block_rows):
    n_experts = cnt_ref.shape[0]
    lanes = bexp_ref.shape[1]
    counts = cnt_ref[...]
    padded = jnp.floor((counts + (block_rows - 0.5)) * (1.0 / block_rows)) * block_rows
    upto = (lax.broadcasted_iota(jnp.int32, (n_experts, n_experts), 1) <=
            lax.broadcasted_iota(jnp.int32, (n_experts, n_experts), 0)).astype(jnp.float32)
    pend = jnp.dot(upto, padded, precision=lax.Precision.HIGHEST, preferred_element_type=jnp.float32)
    poff = pend - padded
    e_iota = lax.broadcasted_iota(jnp.int32, (n_experts, eid_ref.shape[1]), 0)
    for k in range(TOP_K):
        off = jnp.sum(jnp.where(e_iota == eid_ref[k:k + 1, :], poff[:, 0:1], 0.0), axis=0, keepdims=True)
        dest_ref[k:k + 1, :] = off.astype(jnp.int32) + rank_ref[k:k + 1, :]
    n_used = jnp.floor((jnp.max(pend, axis=0, keepdims=True)[:, 0:1] + 0.5) * (1.0 / block_rows))
    b_all = lax.broadcasted_iota(jnp.int32, (1, lanes), 1).astype(jnp.float32)
    start = jnp.minimum(b_all, n_used - 1.0) * block_rows
    ends_before = jnp.where(pend[:, 0:1] <= start, 1.0, 0.0)
    bexp = jnp.minimum(jnp.sum(ends_before, axis=0, keepdims=True), n_experts - 1.0)
    x_iota = lax.broadcasted_iota(jnp.int32, (n_experts, lanes), 0).astype(jnp.float32)
    last_row = jnp.sum(jnp.where(x_iota == bexp, (poff + counts)[:, 0:1], 0.0), axis=0, keepdims=True)
    filled = jnp.clip(last_row - b_all * block_rows, 0.0, float(block_rows))
    bexp_ref[...] = bexp.astype(jnp.int32)
    bcnt_ref[...] = jnp.where(b_all < n_used, filled, 0.0).astype(jnp.int32)
    nused_ref[...] = jnp.broadcast_to(n_used, nused_ref.shape).astype(jnp.int32)


def _dispatch_plan(eid, rank, counts, block_rows):
    top_k, rows = eid.shape
    n_experts = counts.shape[0]
    n_assign = top_k * rows
    n_blocks = -(-(n_assign + n_experts * (block_rows - 1)) // block_rows)
    lanes = -(-n_blocks // V7X_LANES) * V7X_LANES
    tr = _pick_tile(rows, 1536, V7X_LANES) if rows % V7X_LANES == 0 else rows
    kern = functools.partial(_dispatch_kernel, block_rows=block_rows)
    small = lambda n: pl.BlockSpec((1, n), lambda i: (0, 0))
    dest, blk_exp, blk_cnt, n_used = pl.pallas_call(
        kern,
        grid=(rows // tr,),
        in_specs=[pl.BlockSpec((top_k, tr), lambda i: (0, i)),
                  pl.BlockSpec((top_k, tr), lambda i: (0, i)),
                  pl.BlockSpec((n_experts, V7X_LANES), lambda i: (0, 0))],
        out_specs=[pl.BlockSpec((top_k, tr), lambda i: (0, i)), small(lanes), small(lanes), small(V7X_LANES)],
        out_shape=[jax.ShapeDtypeStruct((top_k, rows), jnp.int32),
                   jax.ShapeDtypeStruct((1, lanes), jnp.int32),
                   jax.ShapeDtypeStruct((1, lanes), jnp.int32),
                   jax.ShapeDtypeStruct((1, V7X_LANES), jnp.int32)],
        compiler_params=_params(("arbitrary",), n_experts * tr * 8),
        name="moe_dispatch",
    )(eid, rank, counts)
    tok = jnp.tile(jnp.arange(rows, dtype=jnp.int32), top_k)
    row_tok = jnp.zeros((n_blocks * block_rows,), jnp.int32).at[dest.reshape(n_assign)].set(tok)
    return (row_tok.reshape(n_blocks, 1, block_rows), dest, blk_exp[0, :n_blocks], blk_cnt[0, :n_blocks],
            n_used[0, :1])


def _experts_kernel(exp_ref, cnt_ref, nused_ref, tok0_ref, tok_next_ref, h_ref, wg_hbm, wu_hbm, wd_hbm,
                    o_ref, x_buf, sems, w_buf, w_sems, g_acc, u_acc, h_buf, *, layer):
    b = pl.program_id(0)
    n_used = nused_ref[0]
    used = b < n_used
    slot = b % 2
    p = MOE_PHASE_STEPS
    n_chunks = 3 * p
    d_e = w_buf.shape[1]

    def chunk_copy(blk, c):
        e = exp_ref[blk]
        if c < 2 * p:
            src = (wg_hbm if c % 2 == 0 else wu_hbm).at[layer, e, pl.ds((c // 2) * d_e, d_e), :]
        else:
            src = wd_hbm.at[layer, e, :, pl.ds((c - 2 * p) * d_e, d_e)]
        ring = c % MOE_WEIGHT_SLOTS
        return pltpu.make_async_copy(src, w_buf.at[ring], w_sems.at[ring])

    def row_copy(tok_ref, r, slot_):
        return pltpu.make_async_copy(h_ref.at[pl.ds(tok_ref[0, r], 1)], x_buf.at[slot_, pl.ds(r, 1)],
                                     sems.at[slot_])

    def gather_start(tok_ref, n, slot_):
        def body(grp, c):
            for u in range(DMA_ISSUE_UNROLL):
                row_copy(tok_ref, grp * DMA_ISSUE_UNROLL + u, slot_).start()
            return c

        lax.fori_loop(0, lax.shift_right_logical(n + (DMA_ISSUE_UNROLL - 1), DMA_ISSUE_SHIFT), body, 0)

    def gather_wait(tok_ref, n, slot_):
        def body(grp, c):
            for u in range(DMA_ISSUE_UNROLL):
                row_copy(tok_ref, grp * DMA_ISSUE_UNROLL + u, slot_).wait()
            return c

        lax.fori_loop(0, lax.shift_right_logical(n + (DMA_ISSUE_UNROLL - 1), DMA_ISSUE_SHIFT), body, 0)

    @pl.when(b == 0)
    def _():
        for c in range(MOE_WEIGHT_LOOKAHEAD):
            chunk_copy(0, c).start()
        x_buf[...] = jnp.zeros(x_buf.shape, x_buf.dtype)
        gather_start(tok0_ref, cnt_ref[0], 0)

    @pl.when(used)
    def _():
        gather_wait(tok0_ref, cnt_ref[b], slot)

        @pl.when(b + 1 < n_used)
        def _():
            gather_start(tok_next_ref, cnt_ref[b + 1], 1 - slot)

        for c in range(n_chunks):
            chunk_copy(b, c).wait()
            ahead = c + MOE_WEIGHT_LOOKAHEAD
            if ahead < n_chunks:
                chunk_copy(b, ahead).start()
            else:
                @pl.when(b + 1 < n_used)
                def _(ahead=ahead):
                    chunk_copy(b + 1, ahead - n_chunks).start()

            w = _bf16(w_buf[c % MOE_WEIGHT_SLOTS])
            if c < 2 * p:
                k = c // 2
                x = _bf16(_unpack_rows(x_buf[slot, :, k * (d_e // 2):(k + 1) * (d_e // 2)], d_e))
                part = jnp.dot(x, w, preferred_element_type=jnp.float32)
                acc = g_acc if c % 2 == 0 else u_acc
                if k == 0:
                    acc[...] = part
                else:
                    acc[...] += part
                if c == 2 * p - 1:
                    g = g_acc[...]
                    h_buf[...] = _bf16(g * jax.nn.sigmoid(g) * u_acc[...])
            else:
                n = c - 2 * p
                y = jnp.dot(h_buf[...], w, preferred_element_type=jnp.float32)
                o_ref[:, n * (d_e // 2):(n + 1) * (d_e // 2)] = _pack_rows(y, d_e)


def _experts(h, row_tok, w_gate, w_up, w_down, layer, blk_exp, blk_cnt, n_used):
    d = 2 * h.shape[1]
    d_e = w_gate.shape[3]
    n_blocks, _, block_rows = row_tok.shape
    n_rows = n_blocks * block_rows
    if d_e * MOE_PHASE_STEPS != d or (3 * MOE_PHASE_STEPS) % MOE_WEIGHT_SLOTS:
        raise ValueError("expert weight chunks must be square and fill the ring a whole number of times")
    if block_rows % DMA_ISSUE_UNROLL:
        raise ValueError("expert block rows must be a whole number of DMA issue groups")

    def out_map(b, exp, cnt, nu):
        return (jnp.maximum(jnp.minimum(b, nu[0] - 1), 0), 0)

    def tok_next_map(b, exp, cnt, nu):
        return (jnp.maximum(jnp.minimum(b + 1, nu[0] - 1), 0), 0, 0)

    grid_spec = pltpu.PrefetchScalarGridSpec(
        num_scalar_prefetch=3,
        grid=(n_blocks,),
        in_specs=[pl.BlockSpec((None, 1, block_rows), lambda b, exp, cnt, nu: (0, 0, 0), memory_space=pltpu.SMEM),
                  pl.BlockSpec((None, 1, block_rows), tok_next_map, memory_space=pltpu.SMEM),
                  pl.BlockSpec(memory_space=pl.ANY),
                  pl.BlockSpec(memory_space=pl.ANY),
                  pl.BlockSpec(memory_space=pl.ANY),
                  pl.BlockSpec(memory_space=pl.ANY)],
        out_specs=pl.BlockSpec((block_rows, d // 2), out_map),
        scratch_shapes=[pltpu.VMEM((2, block_rows, d // 2), jnp.uint32),
                        pltpu.SemaphoreType.DMA((2,)),
                        pltpu.VMEM((MOE_WEIGHT_SLOTS, d_e, d_e), jnp.float32),
                        pltpu.SemaphoreType.DMA((MOE_WEIGHT_SLOTS,)),
                        pltpu.VMEM((block_rows, d_e), jnp.float32),
                        pltpu.VMEM((block_rows, d_e), jnp.float32),
                        pltpu.VMEM((block_rows, d_e), MXU_DTYPE)],
    )
    blk_bytes = (block_rows * d * 2 * 2 + MOE_WEIGHT_SLOTS * d_e * d_e * 2 + d_e * d_e * 2 + block_rows * d_e * 9)
    return pl.pallas_call(
        functools.partial(_experts_kernel, layer=layer),
        grid_spec=grid_spec,
        out_shape=jax.ShapeDtypeStruct((n_rows, d // 2), jnp.uint32),
        compiler_params=_params(("arbitrary",), blk_bytes),
        name="moe_experts",
    )(blk_exp, blk_cnt, n_used, row_tok, row_tok, h, w_gate, w_up, w_down)


def _combine_kernel(dest_ref, y_ref, x_ref, wt_ref, g_ref, *refs, tr, total_rows, prompt_steps, pack_chunk,
                    next_norm):
    out_refs, (ybuf, sems) = refs[:-2], refs[-2:]
    i = pl.program_id(0)
    slot = i % 2

    def copy(step, slot_, k, r):
        row = dest_ref[k * total_rows + step * tr + r]
        return pltpu.make_async_copy(y_ref.at[pl.ds(row, 1)], ybuf.at[slot_, k, pl.ds(r, 1)], sems.at[slot_])

    def gather_start(step, slot_):
        def body(grp, c):
            for u in range(DMA_ISSUE_UNROLL):
                for k in range(TOP_K):
                    copy(step, slot_, k, grp * DMA_ISSUE_UNROLL + u).start()
            return c

        lax.fori_loop(0, tr // DMA_ISSUE_UNROLL, body, 0)

    @pl.when(i == 0)
    def _():
        gather_start(0, 0)

    @pl.when(i + 1 < pl.num_programs(0))
    def _():
        gather_start(i + 1, 1 - slot)

    def wait(grp, c):
        for u in range(DMA_ISSUE_UNROLL):
            for k in range(TOP_K):
                copy(i, slot, k, grp * DMA_ISSUE_UNROLL + u).wait()
        return c

    lax.fori_loop(0, tr // DMA_ISSUE_UNROLL, wait, 0)

    def emit(o_ref):
        for r in range(0, tr, V7X_SUBLANES):
            gi = r // V7X_SUBLANES
            rs = slice(r, r + V7X_SUBLANES)
            y = (wt_ref[rs, 0:1] * _unpack_rows(ybuf[slot, 0, rs, :], pack_chunk)
                 + wt_ref[rs, 1:2] * _unpack_rows(ybuf[slot, 1, rs, :], pack_chunk))
            o_ref[rs, :] = x_ref[rs, :] + g_ref[gi:gi + 1, :] * y

    if next_norm:
        ng_ref, nsc_ref, nsh_ref, o_ref, hn_ref = out_refs
        emit(o_ref)
        _norm_mod_kernel(o_ref, ng_ref, nsc_ref, nsh_ref, hn_ref)
    elif prompt_steps is None:
        emit(out_refs[0])
    else:
        pl.when(i < prompt_steps)(lambda: emit(out_refs[0]))
        pl.when(i >= prompt_steps)(lambda: emit(out_refs[1]))


def _combine(yb, dest, wts, x, modg, *, gate_col, split_rows=None, next_norm=None):
    rows, d = x.shape
    extra_in, extra_specs = [], []
    if next_norm is not None:
        tr = _pick_tile(rows, 128, _BF16_ROWS)
        tg = tr // V7X_SUBLANES
        p_steps = None
        n_gain, n_modg, n_shift, n_scale = next_norm
        extra_in = [n_gain.reshape(1, d), n_modg, n_modg]
        extra_specs = [pl.BlockSpec((1, d), lambda i, dst: (0, 0)),
                       pl.BlockSpec((tg, d), lambda i, dst: (i, n_scale)),
                       pl.BlockSpec((tg, d), lambda i, dst: (i, n_shift))]
        out_specs = [pl.BlockSpec((tr, d), lambda i, dst: (i, 0)), pl.BlockSpec((tr, d), lambda i, dst: (i, 0))]
        out_shape = [jax.ShapeDtypeStruct((rows, d), jnp.float32), jax.ShapeDtypeStruct((rows, d), MXU_DTYPE)]
    elif split_rows is None:
        tr = _pick_tile(rows, 128, V7X_SUBLANES)
        p_steps = None
        out_specs = pl.BlockSpec((tr, d), lambda i, dst: (i, 0))
        out_shape = jax.ShapeDtypeStruct((rows, d), jnp.float32)
    else:
        tr = _pick_tile(math.gcd(split_rows, rows - split_rows), 128, V7X_SUBLANES)
        p_steps = split_rows // tr
        out_specs = [pl.BlockSpec((tr, d), lambda i, dst: (jnp.minimum(i, p_steps - 1), 0)),
                     pl.BlockSpec((tr, d), lambda i, dst: (jnp.maximum(i - p_steps, 0), 0))]
        out_shape = [jax.ShapeDtypeStruct((split_rows, d), jnp.float32),
                     jax.ShapeDtypeStruct((rows - split_rows, d), jnp.float32)]
    kern = functools.partial(_combine_kernel, tr=tr, total_rows=rows, prompt_steps=p_steps,
                             pack_chunk=_exact_div(d, MOE_PHASE_STEPS), next_norm=next_norm is not None)
    grid_spec = pltpu.PrefetchScalarGridSpec(
        num_scalar_prefetch=1,
        grid=(rows // tr,),
        in_specs=[pl.BlockSpec(memory_space=pl.ANY),
                  pl.BlockSpec((tr, d), lambda i, dst: (i, 0)),
                  pl.BlockSpec((tr, TOP_K), lambda i, dst: (i, 0)),
                  pl.BlockSpec((tr // V7X_SUBLANES, d), lambda i, dst: (i, gate_col))] + extra_specs,
        out_specs=out_specs,
        scratch_shapes=[pltpu.VMEM((2, TOP_K, tr, d // 2), yb.dtype),
                        pltpu.SemaphoreType.DMA((2,))],
    )
    return pl.pallas_call(
        kern,
        grid_spec=grid_spec,
        out_shape=out_shape,
        compiler_params=_params(("arbitrary",), tr * d * 20),
        name="moe_combine",
    )(dest.reshape(-1), yb, x, wts.T, modg, *extra_in)


def kernel(x_prompt, x_sample, cache_k, cache_v, state_conv, c_prompt, c_sample, router_w, router_b, norm1_g, norm2_g, w_ada, b_ada, w_in, q_norm_g, k_norm_g, attn_sink, w_ao, dw_w, dw_b, cn_g, cn_b, w_co, w_out, moe_w_gate, moe_w_up, moe_w_down):
    batch, seq, d = x_prompt.shape
    dec_batch, t_new, _ = x_sample.shape
    depth = w_in.shape[0]
    window = cache_k.shape[2]
    n_kv = cache_k.shape[3]
    d_kv = n_kv * cache_k.shape[4]
    d_q = w_ao.shape[1]
    d_conv = w_co.shape[1]
    n_experts = router_w.shape[1]
    if q_norm_g.shape[1] != HEAD_DIM or cache_k.shape[4] != HEAD_DIM or dw_w.shape[1] != CONV_WIDTH:
        raise ValueError("unsupported head / conv geometry")
    if d_q != Q_PER_KV * d_kv or n_experts % N_GROUPS:
        raise ValueError("unsupported head / expert grouping")
    if seq % window or t_new % V7X_SUBLANES or t_new > window:
        raise ValueError("unsupported sequence geometry")
    prompt_rows = batch * seq
    sample_rows = dec_batch * t_new
    rows = prompt_rows + sample_rows
    ga_off = d_q + 2 * d_kv + 2 * d_conv

    SH1, SC1, G1, SH2, SC2, G2 = range(6)

    kp, vp, up, ks, vs, us = [], [], [], [], [], []
    def layer_modulation(l):
        return _adaln(c_prompt, c_sample, w_ada, b_ada, l, prompt_groups=seq // V7X_SUBLANES,
                      sample_groups=t_new // V7X_SUBLANES)

    mg = layer_modulation(0)
    h, x = _norm_mod_stack(x_prompt.reshape(prompt_rows, d), x_sample.reshape(sample_rows, d),
                           norm1_g[0], mg, SH1, SC1)
    for l in range(depth):
        proj = _in_proj(h, w_in, l)

        a, k_last, v_last = _attn_prompt(proj, attn_sink[l], q_norm_g[l], k_norm_g[l], batch=batch, seq=seq,
                                         total_rows=rows, d_q=d_q, d_kv=d_kv, window=window)
        a, k_buf, v_buf = _attn_sample(a, proj, cache_k, cache_v, l, attn_sink[l], q_norm_g[l], k_norm_g[l],
                                       prompt_rows=prompt_rows, dec_batch=dec_batch, t_new=t_new,
                                       d_q=d_q, d_kv=d_kv, window=window)
        cv, u_last = _conv_prompt(proj, dw_w[l], dw_b[l], cn_g[l], cn_b[l], batch=batch, seq=seq,
                                  total_rows=rows, d_q=d_q, d_kv=d_kv, d_conv=d_conv)
        cv, u_new = _conv_sample(cv, proj, state_conv, l, dw_w[l], dw_b[l], cn_g[l], cn_b[l],
                                 prompt_rows=prompt_rows, dec_batch=dec_batch, t_new=t_new,
                                 d_q=d_q, d_kv=d_kv, d_conv=d_conv)

        z = _merge(a, cv, proj, w_ao, w_co, l, ga_off=ga_off)
        x = _out_proj(z, w_out, x, mg, l, gate_col=G1)

        h2, eid, wts, rank, counts = _norm_router(x, norm2_g[l], mg, router_w, router_b,
                                                  shift_col=SH2, scale_col=SC2)
        row_tok, dest, blk_exp, blk_cnt, n_used = _dispatch_plan(eid, rank, counts, MOE_BLOCK_ROWS)
        yb = _experts(h2, row_tok, moe_w_gate, moe_w_up, moe_w_down, l, blk_exp, blk_cnt, n_used)
        if l + 1 < depth:
            mg_next = layer_modulation(l + 1)
            x, h = _combine(yb, dest, wts, x, mg, gate_col=G2, next_norm=(norm1_g[l + 1], mg_next, SH1, SC1))
            mg = mg_next
        else:
            y_prompt, y_sample = _combine(yb, dest, wts, x, mg, gate_col=G2, split_rows=prompt_rows)

        kp.append(k_last)
        vp.append(v_last)
        up.append(u_last)
        ks.append(k_buf)
        vs.append(v_buf)
        us.append(u_new)

    def heads(bufs, n_seq):
        return jnp.stack(bufs).reshape(depth, n_seq, window, n_kv, HEAD_DIM)

    y_prompt = y_prompt.reshape(batch, seq, d)
    y_sample = y_sample.reshape(dec_batch, t_new, d)
    return (y_prompt, y_sample, heads(kp, batch), heads(vp, batch), jnp.stack(up),
            heads(ks, dec_batch), heads(vs, dec_batch), jnp.stack(us))
```
